```python
import math
import jax, jax.numpy as jnp
from jax import lax
import numpy as np

D_MODEL = 1024
BATCH = 8
SEQ = 4096
DEPTH = 4

N_MIXERS = 3
N_META = 16
EPS = 1e-6
S5_WIDTH = D_MODEL
S5_GROUP = 16
S5_GROUPS = S5_WIDTH // S5_GROUP
S5_STATE = 64
DT_MIN = 1e-3
DT_MAX = 1e-1
CONV_E = 2 * D_MODEL
CONV_K = 3
POOL_E = 2 * D_MODEL
POOL_WINDOWS = (2, 4, 8, 16)
POOL_GROUP = POOL_E // len(POOL_WINDOWS)

kernel_name = "hybrid_s5_shortconv_pool_interleaved"


def rmsnorm(h, g):
    hf = h.astype(jnp.float32)
    y = hf * lax.rsqrt(jnp.mean(hf * hf, axis=-1, keepdims=True) + EPS)
    return (y * g.astype(jnp.float32)).astype(h.dtype)


def s5_branch(n, w_in, lam_re, lam_im, log_dt, b_re, b_im, c_re, c_im, d_skip, w_glu, b_glu, w_out):
    f32 = jnp.float32
    bsz, L, _ = n.shape
    u, z = jnp.split(n @ w_in, 2, axis=-1)
    uf = u.astype(f32).reshape(bsz, L, S5_GROUPS, S5_GROUP)
    lr = lam_re.astype(f32)
    li = lam_im.astype(f32)
    dt = jnp.exp(log_dt.astype(f32))[:, None]
    mag = jnp.exp(lr * dt)
    ar = mag * jnp.cos(li * dt)
    ai = mag * jnp.sin(li * dt)
    den = lr * lr + li * li
    kr = ((ar - 1.0) * lr + ai * li) / den
    ki = (ai * lr - (ar - 1.0) * li) / den
    br = b_re.astype(f32)
    bi = b_im.astype(f32)
    bbr = kr[..., None] * br - ki[..., None] * bi
    bbi = kr[..., None] * bi + ki[..., None] * br
    xr = jnp.einsum("blgi,gpi->blgp", uf, bbr)
    xi = jnp.einsum("blgi,gpi->blgp", uf, bbi)
    a_r = jnp.broadcast_to(ar[None, None], (1, L, S5_GROUPS, S5_STATE))
    a_i = jnp.broadcast_to(ai[None, None], (1, L, S5_GROUPS, S5_STATE))

    def combine(e1, e2):
        a1r, a1i, b1r, b1i = e1
        a2r, a2i, b2r, b2i = e2
        return (a2r * a1r - a2i * a1i,
                a2r * a1i + a2i * a1r,
                a2r * b1r - a2i * b1i + b2r,
                a2r * b1i + a2i * b1r + b2i)

    _, _, sr, si = lax.associative_scan(combine, (a_r, a_i, xr, xi), axis=1)
    y = (jnp.einsum("blgp,gip->blgi", sr, c_re.astype(f32))
         - jnp.einsum("blgp,gip->blgi", si, c_im.astype(f32))
         + d_skip.astype(f32).reshape(S5_GROUPS, S5_GROUP) * uf)
    y = jax.nn.gelu(y.reshape(bsz, L, S5_WIDTH))
    y = y * jax.nn.sigmoid(y @ w_glu.astype(f32) + b_glu.astype(f32))
    y = y.astype(n.dtype) * jax.nn.silu(z)
    return y @ w_out


def shortconv_branch(n, w_in, conv_w, conv_b, w_out):
    bg, cg, v, z = jnp.split(n @ w_in, 4, axis=-1)
    hc = cg * v
    conv = lax.conv_general_dilated(
        hc, conv_w[:, None, :], window_strides=(1,), padding=[(CONV_K - 1, 0)],
        dimension_numbers=("NWC", "WIO", "NWC"), feature_group_count=CONV_E) + conv_b
    y = bg * conv
    return (y * jax.nn.silu(z)) @ w_out


def pool_branch(n, w_in, w_grp, b_grp, scale, w_out):
    f32 = jnp.float32
    bsz, L, _ = n.shape
    u, z = jnp.split(n @ w_in, 2, axis=-1)
    ug = u.astype(f32).reshape(bsz, L, len(POOL_WINDOWS), POOL_GROUP)
    cs = jnp.cumsum(ug, axis=1)
    t = jnp.arange(1, L + 1, dtype=f32)[:, None]
    outs = []
    for k, w in enumerate(POOL_WINDOWS):
        c = cs[:, :, k]
        lag = jnp.concatenate([jnp.zeros_like(c[:, :w]), c[:, :L - w]], axis=1)
        mixed = (c - lag) / jnp.minimum(t, float(w)) - ug[:, :, k]
        outs.append(mixed @ w_grp[k].astype(f32) + b_grp[k].astype(f32))
    y = jnp.concatenate(outs, axis=-1) * scale.astype(f32)
    y = y.astype(n.dtype) * jax.nn.silu(z)
    return y @ w_out


def _normal(key, shape, std):
    return jax.random.normal(key, shape, jnp.float32) * std


def _s5_params(key, p):
    ks = jax.random.split(key, 12)
    n_idx = jnp.arange(S5_STATE, dtype=jnp.float32)
    return {
        p + "w_in": _normal(ks[0], (D_MODEL, 2 * S5_WIDTH), D_MODEL ** -0.5),
        p + "lam_re": -0.5 + _normal(ks[1], (S5_GROUPS, S5_STATE), 0.01),
        p + "lam_im": math.pi * n_idx[None, :] + _normal(ks[2], (S5_GROUPS, S5_STATE), 0.01),
        p + "log_dt": jax.random.uniform(ks[3], (S5_GROUPS,), jnp.float32,
                                         math.log(DT_MIN), math.log(DT_MAX)),
        p + "b_re": _normal(ks[4], (S5_GROUPS, S5_STATE, S5_GROUP), (2 * S5_GROUP) ** -0.5),
        p + "b_im": _normal(ks[5], (S5_GROUPS, S5_STATE, S5_GROUP), (2 * S5_GROUP) ** -0.5),
        p + "c_re": _normal(ks[6], (S5_GROUPS, S5_GROUP, S5_STATE), (2 * S5_STATE) ** -0.5),
        p + "c_im": _normal(ks[7], (S5_GROUPS, S5_GROUP, S5_STATE), (2 * S5_STATE) ** -0.5),
        p + "d_skip": _normal(ks[8], (S5_WIDTH,), 1.0),
        p + "w_glu": _normal(ks[9], (S5_WIDTH, S5_WIDTH), S5_WIDTH ** -0.5),
        p + "b_glu": _normal(ks[10], (S5_WIDTH,), 0.01),
        p + "w_out": _normal(ks[11], (S5_WIDTH, D_MODEL), S5_WIDTH ** -0.5),
    }


def _conv_params(key, p):
    ks = jax.random.split(key, 4)
    return {
        p + "w_in": _normal(ks[0], (D_MODEL, 4 * CONV_E), D_MODEL ** -0.5),
        p + "conv_w": _normal(ks[1], (CONV_K, CONV_E), CONV_K ** -0.5),
        p + "conv_b": _normal(ks[2], (CONV_E,), 0.01),
        p + "w_out": _normal(ks[3], (CONV_E, D_MODEL), CONV_E ** -0.5),
    }


def _pool_params(key, p):
    ks = jax.random.split(key, 5)
    ng = len(POOL_WINDOWS)
    return {
        p + "w_in": _normal(ks[0], (D_MODEL, 2 * POOL_E), D_MODEL ** -0.5),
        p + "w_grp": _normal(ks[1], (ng, POOL_GROUP, POOL_GROUP), POOL_GROUP ** -0.5),
        p + "b_grp": _normal(ks[2], (ng, POOL_GROUP), 0.01),
        p + "scale": 1.0 + _normal(ks[3], (POOL_E,), 0.02),
        p + "w_out": _normal(ks[4], (POOL_E, D_MODEL), POOL_E ** -0.5),
    }


def setup_inputs(seed: int = 0) -> dict:
    key = jax.random.key(seed)
    ks = jax.random.split(key, 3 + 2 * DEPTH)
    out = {
        "x": _normal(ks[0], (BATCH, SEQ, D_MODEL), 1.0),
        "meta_tokens": _normal(ks[1], (N_META, D_MODEL), 1.0),
    }
    builders = (_s5_params, _conv_params, _pool_params)
    for i in range(DEPTH):
        out["norm%d_g" % i] = 1.0 + _normal(ks[3 + 2 * i], (D_MODEL,), 0.02)
        out.update(builders[i % N_MIXERS](ks[4 + 2 * i], "l%d_" % i))
    out["final_g"] = 1.0 + _normal(ks[2], (D_MODEL,), 0.02)
    return out


def reference(x, meta_tokens,
              norm0_g, l0_w_in, l0_lam_re, l0_lam_im, l0_log_dt, l0_b_re, l0_b_im, l0_c_re, l0_c_im,
              l0_d_skip, l0_w_glu, l0_b_glu, l0_w_out,
              norm1_g, l1_w_in, l1_conv_w, l1_conv_b, l1_w_out,
              norm2_g, l2_w_in, l2_w_grp, l2_b_grp, l2_scale, l2_w_out,
              norm3_g, l3_w_in, l3_lam_re, l3_lam_im, l3_log_dt, l3_b_re, l3_b_im, l3_c_re, l3_c_im,
              l3_d_skip, l3_w_glu, l3_b_glu, l3_w_out,
              final_g):
    bsz = x.shape[0]
    meta = jnp.broadcast_to(meta_tokens[None].astype(x.dtype), (bsz, N_META, D_MODEL))
    h = jnp.concatenate([meta, x], axis=1)
    layers = [
        (norm0_g, (l0_w_in, l0_lam_re, l0_lam_im, l0_log_dt, l0_b_re, l0_b_im, l0_c_re, l0_c_im,
                   l0_d_skip, l0_w_glu, l0_b_glu, l0_w_out)),
        (norm1_g, (l1_w_in, l1_conv_w, l1_conv_b, l1_w_out)),
        (norm2_g, (l2_w_in, l2_w_grp, l2_b_grp, l2_scale, l2_w_out)),
        (norm3_g, (l3_w_in, l3_lam_re, l3_lam_im, l3_log_dt, l3_b_re, l3_b_im, l3_c_re, l3_c_im,
                   l3_d_skip, l3_w_glu, l3_b_glu, l3_w_out)),
    ]
    mixers = (s5_branch, shortconv_branch, pool_branch)
    for i in range(DEPTH):
        g, params = layers[i]
        h = h + mixers[i % N_MIXERS](rmsnorm(h, g), *params)
    return rmsnorm(h[:, N_META:], final_g)
```

```python
import functools

import jax
import jax.numpy as jnp
from jax import lax
from jax.experimental import pallas as pl
from jax.experimental.pallas import tpu as pltpu

D_MODEL = 1024
BATCH = 8
SEQ = 4096
N_META = 16
EPS = 1e-6
S5_GROUP = 16
S5_GROUPS = 64
S5_STATE = 64
CONV_E = 2048
POOL_E = 2048
POOL_WINDOWS = (2, 4, 8, 16)
POOL_GROUP = 512

TQ = 64
ROWS = TQ * BATCH
PAD = TQ - N_META
L_PAD = SEQ + TQ
N_CHUNKS = L_PAD // TQ
S5_BLOCKS = 4
S5_BLOCK_CH = D_MODEL // S5_BLOCKS
S5_BLOCK_ST = S5_BLOCK_CH * S5_STATE // S5_GROUP
HIST = 16 * BATCH
CONV_HIST = 2 * BATCH
VMEM_LIMIT = 56 * 1024 * 1024

F32 = jnp.float32
BF16 = jnp.bfloat16


def _rms(h, g):
    ms = jnp.mean(h * h, axis=-1, keepdims=True)
    return h * lax.rsqrt(ms + EPS) * g


def _dot(a, b):
    return jnp.dot(a, b, preferred_element_type=F32)


def _s5_kernel(h_ref, g_ref, win_ref, bw_ref, cw_ref, ar_ref, ai_ref, dsk_ref,
               wglu_ref, bglu_ref, wout_ref, fg_ref, o_ref, xs_ref, carry_ref, *, final):
    @pl.when(pl.program_id(0) == 0)
    def _():
        carry_ref[...] = jnp.zeros_like(carry_ref)

    h = h_ref[...]
    nb = _rms(h, g_ref[...]).astype(BF16)
    uz = _dot(nb, win_ref[...])
    u = uz[:, :D_MODEL]
    z = uz[:, D_MODEL:]
    ub = u.astype(BF16)

    ys = []
    for blk in range(S5_BLOCKS):
        xs_ref[...] = _dot(ub[:, blk * S5_BLOCK_CH:(blk + 1) * S5_BLOCK_CH], bw_ref[blk])
        ar = jnp.broadcast_to(ar_ref[blk:blk + 1, :], (BATCH, S5_BLOCK_ST))
        ai = jnp.broadcast_to(ai_ref[blk:blk + 1, :], (BATCH, S5_BLOCK_ST))

        def step(t, c):
            sr, si = c
            r = pl.multiple_of(t * BATCH, BATCH)
            xr = xs_ref[pl.ds(r, BATCH), 0:S5_BLOCK_ST]
            xi = xs_ref[pl.ds(r, BATCH), S5_BLOCK_ST:2 * S5_BLOCK_ST]
            nr = ar * sr - ai * si + xr
            ni = ar * si + ai * sr + xi
            xs_ref[pl.ds(r, BATCH), 0:S5_BLOCK_ST] = nr
            xs_ref[pl.ds(r, BATCH), S5_BLOCK_ST:2 * S5_BLOCK_ST] = ni
            return nr, ni

        sr, si = lax.fori_loop(0, TQ, step, (carry_ref[blk, 0], carry_ref[blk, 1]), unroll=8)
        carry_ref[blk, 0] = sr
        carry_ref[blk, 1] = si
        ys.append(_dot(xs_ref[...].astype(BF16), cw_ref[blk]))

    y = jnp.concatenate(ys, axis=-1) + dsk_ref[...] * u
    y = jax.nn.gelu(y)
    y = y * jax.nn.sigmoid(_dot(y.astype(BF16), wglu_ref[...]) + bglu_ref[...])
    y = y * jax.nn.silu(z)
    hn = h + _dot(y.astype(BF16), wout_ref[...])
    if final:
        hn = _rms(hn, fg_ref[...])
    o_ref[...] = hn


def _conv_kernel(h_ref, g_ref, win_ref, cw_ref, cb_ref, wout_ref, o_ref, hc_ref, y_ref):
    @pl.when(pl.program_id(0) == 0)
    def _():
        hc_ref[0:CONV_HIST, :] = jnp.zeros((CONV_HIST, CONV_E), F32)

    h = h_ref[...]
    nb = _rms(h, g_ref[...]).astype(BF16)
    cw = 512
    for c in range(CONV_E // cw):
        lo = c * cw
        bg = _dot(nb, win_ref[:, lo:lo + cw])
        cg = _dot(nb, win_ref[:, CONV_E + lo:CONV_E + lo + cw])
        v = _dot(nb, win_ref[:, 2 * CONV_E + lo:2 * CONV_E + lo + cw])
        z = _dot(nb, win_ref[:, 3 * CONV_E + lo:3 * CONV_E + lo + cw])
        hc = cg * v
        hc_ref[CONV_HIST:CONV_HIST + ROWS, lo:lo + cw] = hc
        conv = (cw_ref[2:3, lo:lo + cw] * hc
                + cw_ref[1:2, lo:lo + cw] * hc_ref[BATCH:BATCH + ROWS, lo:lo + cw]
                + cw_ref[0:1, lo:lo + cw] * hc_ref[0:ROWS, lo:lo + cw]
                + cb_ref[:, lo:lo + cw])
        hc_ref[0:CONV_HIST, lo:lo + cw] = hc[ROWS - CONV_HIST:ROWS]
        y_ref[:, lo:lo + cw] = (bg * conv * jax.nn.silu(z)).astype(BF16)
    o_ref[...] = h + _dot(y_ref[...], wout_ref[...])


def _window_sum(ref, lo, width, w):
    a = ref[HIST - (w - 1) * BATCH:HIST + ROWS, lo:lo + width]
    span = 1
    while span < w:
        sh = span * BATCH
        a = a[sh:] + a[:-sh]
        span *= 2
    return a


def _pool_kernel(h_ref, g_ref, win_ref, wgrp_ref, bgrp_ref, scale_ref, wout_ref, o_ref, ub_ref, y_ref):
    i = pl.program_id(0)

    @pl.when(i == 0)
    def _():
        ub_ref[0:HIST, :] = jnp.zeros((HIST, POOL_E), F32)

    h = h_ref[...]
    nb = _rms(h, g_ref[...]).astype(BF16)
    pos = i * TQ - PAD + 1 + (lax.broadcasted_iota(jnp.int32, (ROWS, POOL_GROUP), 0) // BATCH)
    for k, w in enumerate(POOL_WINDOWS):
        lo = k * POOL_GROUP
        u = _dot(nb, win_ref[:, lo:lo + POOL_GROUP])
        z = _dot(nb, win_ref[:, POOL_E + lo:POOL_E + lo + POOL_GROUP])
        ub_ref[HIST:HIST + ROWS, lo:lo + POOL_GROUP] = u
        ws = _window_sum(ub_ref, lo, POOL_GROUP, w)
        ub_ref[0:HIST, lo:lo + POOL_GROUP] = u[ROWS - HIST:ROWS]
        cnt = jnp.clip(pos, 1, w).astype(F32)
        mixed = ws / cnt - u
        o = _dot(mixed.astype(BF16), wgrp_ref[k]) + bgrp_ref[k:k + 1, :]
        y = o * scale_ref[:, lo:lo + POOL_GROUP]
        y_ref[:, lo:lo + POOL_GROUP] = (y * jax.nn.silu(z)).astype(BF16)
    o_ref[...] = h + _dot(y_ref[...], wout_ref[...])


def _const_spec(shape):
    nd = len(shape)
    return pl.BlockSpec(shape, lambda i: (0,) * nd, pipeline_mode=pl.Buffered(1))


def _row_spec():
    return pl.BlockSpec((ROWS, D_MODEL), lambda i: (i, 0))


def _call(body, name, h, consts, scratch):
    return pl.pallas_call(
        body,
        out_shape=jax.ShapeDtypeStruct(h.shape, h.dtype),
        grid=(N_CHUNKS,),
        in_specs=[_row_spec()] + [_const_spec(c.shape) for c in consts],
        out_specs=_row_spec(),
        scratch_shapes=scratch,
        compiler_params=pltpu.CompilerParams(
            dimension_semantics=("arbitrary",), vmem_limit_bytes=VMEM_LIMIT),
        name=name,
    )(h, *consts)


def _row(v):
    return v.astype(F32).reshape(1, -1)


def _s5_weights(lam_re, lam_im, log_dt, b_re, b_im, c_re, c_im):
    lr = lam_re.astype(F32)
    li = lam_im.astype(F32)
    dt = jnp.exp(log_dt.astype(F32))[:, None]
    mag = jnp.exp(lr * dt)
    ar = mag * jnp.cos(li * dt)
    ai = mag * jnp.sin(li * dt)
    den = lr * lr + li * li
    kr = ((ar - 1.0) * lr + ai * li) / den
    ki = (ai * lr - (ar - 1.0) * li) / den
    br = b_re.astype(F32)
    bi = b_im.astype(F32)
    bbr = kr[..., None] * br - ki[..., None] * bi
    bbi = kr[..., None] * bi + ki[..., None] * br
    gpb = S5_GROUPS // S5_BLOCKS
    eye = jnp.eye(gpb, dtype=F32)

    def bmat(bb):
        t = jnp.einsum("bgpi,gh->bgihp", bb.reshape(S5_BLOCKS, gpb, S5_STATE, S5_GROUP), eye)
        return t.reshape(S5_BLOCKS, S5_BLOCK_CH, S5_BLOCK_ST)

    def cmat(cc):
        t = jnp.einsum("bgip,gh->bgphi", cc.reshape(S5_BLOCKS, gpb, S5_GROUP, S5_STATE), eye)
        return t.reshape(S5_BLOCKS, S5_BLOCK_ST, S5_BLOCK_CH)

    bw = jnp.concatenate([bmat(bbr), bmat(bbi)], axis=2).astype(BF16)
    cw = jnp.concatenate([cmat(c_re.astype(F32)), -cmat(c_im.astype(F32))], axis=1).astype(BF16)
    return bw, cw, ar.reshape(S5_BLOCKS, S5_BLOCK_ST), ai.reshape(S5_BLOCKS, S5_BLOCK_ST)


def _s5_layer(h, g, w_in, lam_re, lam_im, log_dt, b_re, b_im, c_re, c_im, d_skip, w_glu, b_glu, w_out,
              final_g, name):
    bw, cw, ar, ai = _s5_weights(lam_re, lam_im, log_dt, b_re, b_im, c_re, c_im)
    final = final_g is not None
    fg = _row(final_g) if final else jnp.ones((1, D_MODEL), F32)
    consts = [_row(g), w_in.astype(BF16), bw, cw, ar, ai, _row(d_skip), w_glu.astype(BF16), _row(b_glu),
              w_out.astype(BF16), fg]
    scratch = [pltpu.VMEM((ROWS, 2 * S5_BLOCK_ST), F32),
               pltpu.VMEM((S5_BLOCKS, 2, BATCH, S5_BLOCK_ST), F32)]
    return _call(functools.partial(_s5_kernel, final=final), name, h, consts, scratch)


def _conv_layer(h, g, w_in, conv_w, conv_b, w_out, name):
    consts = [_row(g), w_in.astype(BF16), conv_w.astype(F32), _row(conv_b), w_out.astype(BF16)]
    scratch = [pltpu.VMEM((CONV_HIST + ROWS, CONV_E), F32), pltpu.VMEM((ROWS, CONV_E), BF16)]
    return _call(_conv_kernel, name, h, consts, scratch)


def _pool_layer(h, g, w_in, w_grp, b_grp, scale, w_out, name):
    consts = [_row(g), w_in.astype(BF16), w_grp.astype(BF16), b_grp.astype(F32), _row(scale),
              w_out.astype(BF16)]
    scratch = [pltpu.VMEM((HIST + ROWS, POOL_E), F32), pltpu.VMEM((ROWS, POOL_E), BF16)]
    return _call(_pool_kernel, name, h, consts, scratch)


def kernel(x, meta_tokens, norm0_g, l0_w_in, l0_lam_re, l0_lam_im, l0_log_dt, l0_b_re, l0_b_im, l0_c_re, l0_c_im, l0_d_skip, l0_w_glu, l0_b_glu, l0_w_out, norm1_g, l1_w_in, l1_conv_w, l1_conv_b, l1_w_out, norm2_g, l2_w_in, l2_w_grp, l2_b_grp, l2_scale, l2_w_out, norm3_g, l3_w_in, l3_lam_re, l3_lam_im, l3_log_dt, l3_b_re, l3_b_im, l3_c_re, l3_c_im, l3_d_skip, l3_w_glu, l3_b_glu, l3_w_out, final_g):
    assert x.shape == (BATCH, SEQ, D_MODEL) and meta_tokens.shape == (N_META, D_MODEL)
    meta = jnp.broadcast_to(meta_tokens.astype(x.dtype)[:, None, :], (N_META, BATCH, D_MODEL))
    h = jnp.concatenate([jnp.zeros((PAD, BATCH, D_MODEL), x.dtype), meta, jnp.transpose(x, (1, 0, 2))], axis=0)
    h = h.reshape(L_PAD * BATCH, D_MODEL)
    h = _s5_layer(h, norm0_g, l0_w_in, l0_lam_re, l0_lam_im, l0_log_dt, l0_b_re, l0_b_im, l0_c_re, l0_c_im,
                  l0_d_skip, l0_w_glu, l0_b_glu, l0_w_out, None, "s5_layer0")
    h = _conv_layer(h, norm1_g, l1_w_in, l1_conv_w, l1_conv_b, l1_w_out, "conv_layer1")
    h = _pool_layer(h, norm2_g, l2_w_in, l2_w_grp, l2_b_grp, l2_scale, l2_w_out, "pool_layer2")
    h = _s5_layer(h, norm3_g, l3_w_in, l3_lam_re, l3_lam_im, l3_log_dt, l3_b_re, l3_b_im, l3_c_re, l3_c_im,
                  l3_d_skip, l3_w_glu, l3_b_glu, l3_w_out, final_g, "s5_layer3")
    out = h.reshape(L_PAD, BATCH, D_MODEL)[TQ:]
    return jnp.transpose(out, (1, 0, 2))
```

```python
import functools

import jax
import jax.numpy as jnp
from jax import lax
from jax.experimental import pallas as pl
from jax.experimental.pallas import tpu as pltpu

D_MODEL = 1024
BATCH = 8
SEQ = 4096
N_META = 16
EPS = 1e-6
S5_GROUP = 16
S5_GROUPS = 64
S5_STATE = 64
CONV_E = 2048
POOL_E = 2048
POOL_WINDOWS = (2, 4, 8, 16)
POOL_GROUP = 512

TQ = 64
ROWS = TQ * BATCH
PAD = TQ - N_META
L_PAD = SEQ + TQ
N_CHUNKS = L_PAD // TQ
S5_BLOCKS = 4
S5_BLOCK_CH = D_MODEL // S5_BLOCKS
S5_BLOCK_ST = S5_BLOCK_CH * S5_STATE // S5_GROUP
HIST = 16 * BATCH
CONV_HIST = 2 * BATCH
VMEM_LIMIT = 56 * 1024 * 1024

F32 = jnp.float32
BF16 = jnp.bfloat16


def _rms(h, g):
    ms = jnp.mean(h * h, axis=-1, keepdims=True)
    return h * lax.rsqrt(ms + EPS) * g


def _dot(a, b):
    return jnp.dot(a, b, preferred_element_type=F32)


def _s5_kernel(*refs, first, final):
    if first:
        head_ref, refs = refs[0], refs[1:]
    (h_ref, g_ref, win_ref, bw_ref, cw_ref, ar_ref, ai_ref, dsk_ref,
     wglu_ref, bglu_ref, wout_ref, fg_ref, o_ref, xs_ref, carry_ref) = refs
    i = pl.program_id(0)

    @pl.when(i == 0)
    def _():
        carry_ref[...] = jnp.zeros_like(carry_ref)

    h = h_ref[...]
    if first:
        h = jnp.where(i == 0, head_ref[...], h)
    nb = _rms(h, g_ref[...]).astype(BF16)
    uz = _dot(nb, win_ref[...])
    u = uz[:, :D_MODEL]
    z = uz[:, D_MODEL:]
    ub = u.astype(BF16)

    ys = []
    for blk in range(S5_BLOCKS):
        xs = xs_ref.at[blk % 2]
        xs[...] = _dot(ub[:, blk * S5_BLOCK_CH:(blk + 1) * S5_BLOCK_CH], bw_ref[blk])
        ar = jnp.broadcast_to(ar_ref[blk:blk + 1, :], (BATCH, S5_BLOCK_ST))
        ai = jnp.broadcast_to(ai_ref[blk:blk + 1, :], (BATCH, S5_BLOCK_ST))
        sr = carry_ref[blk, 0]
        si = carry_ref[blk, 1]
        for t in range(TQ):
            r = t * BATCH
            xr = xs[r:r + BATCH, 0:S5_BLOCK_ST]
            xi = xs[r:r + BATCH, S5_BLOCK_ST:2 * S5_BLOCK_ST]
            sr, si = ar * sr - ai * si + xr, ar * si + ai * sr + xi
            xs[r:r + BATCH, 0:S5_BLOCK_ST] = sr
            xs[r:r + BATCH, S5_BLOCK_ST:2 * S5_BLOCK_ST] = si
        carry_ref[blk, 0] = sr
        carry_ref[blk, 1] = si
        ys.append(_dot(xs[...].astype(BF16), cw_ref[blk]))

    y = jnp.concatenate(ys, axis=-1) + dsk_ref[...] * u
    y = jax.nn.gelu(y)
    y = y * jax.nn.sigmoid(_dot(y.astype(BF16), wglu_ref[...]) + bglu_ref[...])
    y = y * jax.nn.silu(z)
    hn = h + _dot(y.astype(BF16), wout_ref[...])
    if final:
        hn = _rms(hn, fg_ref[...])
    o_ref[...] = hn


def _conv_kernel(h_ref, g_ref, win_ref, cw_ref, cb_ref, wout_ref, o_ref, hc_ref, y_ref):
    @pl.when(pl.program_id(0) == 0)
    def _():
        hc_ref[0:CONV_HIST, :] = jnp.zeros((CONV_HIST, CONV_E), F32)

    h = h_ref[...]
    nb = _rms(h, g_ref[...]).astype(BF16)
    cw = 512
    for c in range(CONV_E // cw):
        lo = c * cw
        bg = _dot(nb, win_ref[:, lo:lo + cw])
        cg = _dot(nb, win_ref[:, CONV_E + lo:CONV_E + lo + cw])
        v = _dot(nb, win_ref[:, 2 * CONV_E + lo:2 * CONV_E + lo + cw])
        z = _dot(nb, win_ref[:, 3 * CONV_E + lo:3 * CONV_E + lo + cw])
        hc = cg * v
        hc_ref[CONV_HIST:CONV_HIST + ROWS, lo:lo + cw] = hc
        conv = (cw_ref[2:3, lo:lo + cw] * hc
                + cw_ref[1:2, lo:lo + cw] * hc_ref[BATCH:BATCH + ROWS, lo:lo + cw]
                + cw_ref[0:1, lo:lo + cw] * hc_ref[0:ROWS, lo:lo + cw]
                + cb_ref[:, lo:lo + cw])
        hc_ref[0:CONV_HIST, lo:lo + cw] = hc[ROWS - CONV_HIST:ROWS]
        y_ref[:, lo:lo + cw] = (bg * conv * jax.nn.silu(z)).astype(BF16)
    o_ref[...] = h + _dot(y_ref[...], wout_ref[...])


def _window_sum(ref, lo, width, w):
    a = ref[HIST - (w - 1) * BATCH:HIST + ROWS, lo:lo + width]
    span = 1
    while span < w:
        sh = span * BATCH
        a = a[sh:] + a[:-sh]
        span *= 2
    return a


def _pool_kernel(h_ref, g_ref, win_ref, wgrp_ref, bgrp_ref, scale_ref, wout_ref, o_ref, ub_ref, y_ref):
    i = pl.program_id(0)

    @pl.when(i == 0)
    def _():
        ub_ref[0:HIST, :] = jnp.zeros((HIST, POOL_E), F32)

    h = h_ref[...]
    nb = _rms(h, g_ref[...]).astype(BF16)
    pos = i * TQ - PAD + 1 + (lax.broadcasted_iota(jnp.int32, (ROWS, POOL_GROUP), 0) // BATCH)
    for k, w in enumerate(POOL_WINDOWS):
        lo = k * POOL_GROUP
        u = _dot(nb, win_ref[:, lo:lo + POOL_GROUP])
        z = _dot(nb, win_ref[:, POOL_E + lo:POOL_E + lo + POOL_GROUP])
        ub_ref[HIST:HIST + ROWS, lo:lo + POOL_GROUP] = u
        ws = _window_sum(ub_ref, lo, POOL_GROUP, w)
        ub_ref[0:HIST, lo:lo + POOL_GROUP] = u[ROWS - HIST:ROWS]
        cnt = jnp.clip(pos, 1, w).astype(F32)
        mixed = ws / cnt - u
        o = _dot(mixed.astype(BF16), wgrp_ref[k]) + bgrp_ref[k:k + 1, :]
        y = o * scale_ref[:, lo:lo + POOL_GROUP]
        y_ref[:, lo:lo + POOL_GROUP] = (y * jax.nn.silu(z)).astype(BF16)
    o_ref[...] = h + _dot(y_ref[...], wout_ref[...])


def _const_spec(shape):
    nd = len(shape)
    return pl.BlockSpec(shape, lambda i: (0,) * nd, pipeline_mode=pl.Buffered(1))


def _padded_rows(i):
    return (i, 0)


def _unpadded_rows(i):
    return (jnp.maximum(i - 1, 0), 0)


def _call(body, name, h, consts, scratch, *, head=None, in_rows=_padded_rows, out_rows=_padded_rows):
    n_out = (L_PAD if out_rows is _padded_rows else SEQ) * BATCH
    pre = [] if head is None else [head]
    return pl.pallas_call(
        body,
        out_shape=jax.ShapeDtypeStruct((n_out, D_MODEL), h.dtype),
        grid=(N_CHUNKS,),
        in_specs=([_const_spec(p.shape) for p in pre] + [pl.BlockSpec((ROWS, D_MODEL), in_rows)]
                  + [_const_spec(c.shape) for c in consts]),
        out_specs=pl.BlockSpec((ROWS, D_MODEL), out_rows),
        scratch_shapes=scratch,
        compiler_params=pltpu.CompilerParams(
            dimension_semantics=("arbitrary",), vmem_limit_bytes=VMEM_LIMIT),
        name=name,
    )(*pre, h, *consts)


def _row(v):
    return v.astype(F32).reshape(1, -1)


def _s5_weights(lam_re, lam_im, log_dt, b_re, b_im, c_re, c_im):
    lr = lam_re.astype(F32)
    li = lam_im.astype(F32)
    dt = jnp.exp(log_dt.astype(F32))[:, None]
    mag = jnp.exp(lr * dt)
    ar = mag * jnp.cos(li * dt)
    ai = mag * jnp.sin(li * dt)
    den = lr * lr + li * li
    kr = ((ar - 1.0) * lr + ai * li) / den
    ki = (ai * lr - (ar - 1.0) * li) / den
    br = b_re.astype(F32)
    bi = b_im.astype(F32)
    bbr = kr[..., None] * br - ki[..., None] * bi
    bbi = kr[..., None] * bi + ki[..., None] * br
    gpb = S5_GROUPS // S5_BLOCKS
    eye = jnp.eye(gpb, dtype=F32)

    def bmat(bb):
        t = jnp.einsum("bgpi,gh->bgihp", bb.reshape(S5_BLOCKS, gpb, S5_STATE, S5_GROUP), eye)
        return t.reshape(S5_BLOCKS, S5_BLOCK_CH, S5_BLOCK_ST)

    def cmat(cc):
        t = jnp.einsum("bgip,gh->bgphi", cc.reshape(S5_BLOCKS, gpb, S5_GROUP, S5_STATE), eye)
        return t.reshape(S5_BLOCKS, S5_BLOCK_ST, S5_BLOCK_CH)

    bw = jnp.concatenate([bmat(bbr), bmat(bbi)], axis=2).astype(BF16)
    cw = jnp.concatenate([cmat(c_re.astype(F32)), -cmat(c_im.astype(F32))], axis=1).astype(BF16)
    return bw, cw, ar.reshape(S5_BLOCKS, S5_BLOCK_ST), ai.reshape(S5_BLOCKS, S5_BLOCK_ST)


def _s5_layer(h, g, w_in, lam_re, lam_im, log_dt, b_re, b_im, c_re, c_im, d_skip, w_glu, b_glu, w_out,
              name, *, head=None, final_g=None):
    bw, cw, ar, ai = _s5_weights(lam_re, lam_im, log_dt, b_re, b_im, c_re, c_im)
    first = head is not None
    final = final_g is not None
    fg = _row(final_g) if final else jnp.ones((1, D_MODEL), F32)
    consts = [_row(g), w_in.astype(BF16), bw, cw, ar, ai, _row(d_skip), w_glu.astype(BF16), _row(b_glu),
              w_out.astype(BF16), fg]
    scratch = [pltpu.VMEM((2, ROWS, 2 * S5_BLOCK_ST), F32),
               pltpu.VMEM((S5_BLOCKS, 2, BATCH, S5_BLOCK_ST), F32)]
    return _call(functools.partial(_s5_kernel, first=first, final=final), name, h, consts, scratch, head=head,
                 in_rows=_unpadded_rows if first else _padded_rows,
                 out_rows=_unpadded_rows if final else _padded_rows)


def _conv_layer(h, g, w_in, conv_w, conv_b, w_out, name):
    consts = [_row(g), w_in.astype(BF16), conv_w.astype(F32), _row(conv_b), w_out.astype(BF16)]
    scratch = [pltpu.VMEM((CONV_HIST + ROWS, CONV_E), F32), pltpu.VMEM((ROWS, CONV_E), BF16)]
    return _call(_conv_kernel, name, h, consts, scratch)


def _pool_layer(h, g, w_in, w_grp, b_grp, scale, w_out, name):
    consts = [_row(g), w_in.astype(BF16), w_grp.astype(BF16), b_grp.astype(F32), _row(scale),
              w_out.astype(BF16)]
    scratch = [pltpu.VMEM((HIST + ROWS, POOL_E), F32), pltpu.VMEM((ROWS, POOL_E), BF16)]
    return _call(_pool_kernel, name, h, consts, scratch)


def kernel(x, meta_tokens, norm0_g, l0_w_in, l0_lam_re, l0_lam_im, l0_log_dt, l0_b_re, l0_b_im, l0_c_re, l0_c_im, l0_d_skip, l0_w_glu, l0_b_glu, l0_w_out, norm1_g, l1_w_in, l1_conv_w, l1_conv_b, l1_w_out, norm2_g, l2_w_in, l2_w_grp, l2_b_grp, l2_scale, l2_w_out, norm3_g, l3_w_in, l3_lam_re, l3_lam_im, l3_log_dt, l3_b_re, l3_b_im, l3_c_re, l3_c_im, l3_d_skip, l3_w_glu, l3_b_glu, l3_w_out, final_g):
    assert x.shape == (BATCH, SEQ, D_MODEL) and meta_tokens.shape == (N_META, D_MODEL)
    meta = jnp.broadcast_to(meta_tokens.astype(x.dtype)[:, None, :], (N_META, BATCH, D_MODEL))
    head = jnp.concatenate([jnp.zeros((PAD, BATCH, D_MODEL), x.dtype), meta], axis=0).reshape(ROWS, D_MODEL)
    xt = jnp.transpose(x, (1, 0, 2)).reshape(SEQ * BATCH, D_MODEL)
    h = _s5_layer(xt, norm0_g, l0_w_in, l0_lam_re, l0_lam_im, l0_log_dt, l0_b_re, l0_b_im, l0_c_re, l0_c_im,
                  l0_d_skip, l0_w_glu, l0_b_glu, l0_w_out, "s5_layer0", head=head)
    h = _conv_layer(h, norm1_g, l1_w_in, l1_conv_w, l1_conv_b, l1_w_out, "conv_layer1")
    h = _pool_layer(h, norm2_g, l2_w_in, l2_w_grp, l2_b_grp, l2_scale, l2_w_out, "pool_layer2")
    out = _s5_layer(h, norm3_g, l3_w_in, l3_lam_re, l3_lam_im, l3_log_dt, l3_b_re, l3_b_im, l3_c_re, l3_c_im,
                    l3_d_skip, l3_w_glu, l3_b_glu, l3_w_out, "s5_layer3", final_g=final_g)
    return jnp.transpose(out.reshape(SEQ, BATCH, D_MODEL), (1, 0, 2))
```

```python
import functools

import jax
import jax.numpy as jnp
from jax import lax
from jax.experimental import pallas as pl
from jax.experimental.pallas import tpu as pltpu

D_MODEL = 1024
BATCH = 8
SEQ = 4096
N_META = 16
EPS = 1e-6
S5_GROUP = 16
S5_GROUPS = 64
S5_STATE = 64
CONV_E = 2048
POOL_E = 2048
POOL_WINDOWS = (2, 4, 8, 16)
POOL_GROUP = 512

TQ = 64
ROWS = TQ * BATCH
PAD = TQ - N_META
L_PAD = SEQ + TQ
N_CHUNKS = L_PAD // TQ
S5_BLOCKS = 4
S5_BLOCK_CH = D_MODEL // S5_BLOCKS
S5_BLOCK_ST = S5_BLOCK_CH * S5_STATE // S5_GROUP
HIST = 16 * BATCH
CONV_HIST = 2 * BATCH
VMEM_LIMIT = 56 * 1024 * 1024

F32 = jnp.float32
BF16 = jnp.bfloat16


def _rms(h, g):
    ms = jnp.mean(h * h, axis=-1, keepdims=True)
    return h * lax.rsqrt(ms + EPS) * g


def _dot(a, b):
    return jnp.dot(a, b, preferred_element_type=F32)


def _s5_kernel(*refs, first, final):
    if first:
        head_ref, refs = refs[0], refs[1:]
    (h_ref, g_ref, win_ref, bw_ref, cw_ref, ar_ref, ai_ref, dsk_ref,
     wglu_ref, bglu_ref, wout_ref, fg_ref, o_ref, xs_ref, carry_ref) = refs
    i = pl.program_id(0)

    @pl.when(i == 0)
    def _():
        carry_ref[...] = jnp.zeros_like(carry_ref)

    if first:
        xt = jnp.swapaxes(h_ref[...], 0, 1).reshape(ROWS, D_MODEL)
        h = jnp.where(i == 0, head_ref[...], xt)
    else:
        h = h_ref[...]
    nb = _rms(h, g_ref[...]).astype(BF16)
    uz = _dot(nb, win_ref[...])
    u = uz[:, :D_MODEL]
    z = uz[:, D_MODEL:]
    ub = u.astype(BF16)

    ys = []
    for blk in range(S5_BLOCKS):
        xs = xs_ref.at[blk % 2]
        xs[...] = _dot(ub[:, blk * S5_BLOCK_CH:(blk + 1) * S5_BLOCK_CH], bw_ref[blk])
        ar = jnp.broadcast_to(ar_ref[blk:blk + 1, :], (BATCH, S5_BLOCK_ST))
        ai = jnp.broadcast_to(ai_ref[blk:blk + 1, :], (BATCH, S5_BLOCK_ST))
        sr = carry_ref[blk, 0]
        si = carry_ref[blk, 1]
        for t in range(TQ):
            r = t * BATCH
            xr = xs[r:r + BATCH, 0:S5_BLOCK_ST]
            xi = xs[r:r + BATCH, S5_BLOCK_ST:2 * S5_BLOCK_ST]
            sr, si = ar * sr - ai * si + xr, ar * si + ai * sr + xi
            xs[r:r + BATCH, 0:S5_BLOCK_ST] = sr
            xs[r:r + BATCH, S5_BLOCK_ST:2 * S5_BLOCK_ST] = si
        carry_ref[blk, 0] = sr
        carry_ref[blk, 1] = si
        ys.append(_dot(xs[...].astype(BF16), cw_ref[blk]))

    y = jnp.concatenate(ys, axis=-1) + dsk_ref[...] * u
    y = jax.nn.gelu(y)
    y = y * jax.nn.sigmoid(_dot(y.astype(BF16), wglu_ref[...]) + bglu_ref[...])
    y = y * jax.nn.silu(z)
    hn = h + _dot(y.astype(BF16), wout_ref[...])
    if final:
        hn = _rms(hn, fg_ref[...])
        o_ref[...] = jnp.swapaxes(hn.reshape(TQ, BATCH, D_MODEL), 0, 1)
    else:
        o_ref[...] = hn


def _conv_kernel(h_ref, g_ref, win_ref, cw_ref, cb_ref, wout_ref, o_ref, hc_ref, y_ref):
    @pl.when(pl.program_id(0) == 0)
    def _():
        hc_ref[0:CONV_HIST, :] = jnp.zeros((CONV_HIST, CONV_E), F32)

    h = h_ref[...]
    nb = _rms(h, g_ref[...]).astype(BF16)
    cw = 512
    for c in range(CONV_E // cw):
        lo = c * cw
        bg = _dot(nb, win_ref[:, lo:lo + cw])
        cg = _dot(nb, win_ref[:, CONV_E + lo:CONV_E + lo + cw])
        v = _dot(nb, win_ref[:, 2 * CONV_E + lo:2 * CONV_E + lo + cw])
        z = _dot(nb, win_ref[:, 3 * CONV_E + lo:3 * CONV_E + lo + cw])
        hc = cg * v
        hc_ref[CONV_HIST:CONV_HIST + ROWS, lo:lo + cw] = hc
        conv = (cw_ref[2:3, lo:lo + cw] * hc
                + cw_ref[1:2, lo:lo + cw] * hc_ref[BATCH:BATCH + ROWS, lo:lo + cw]
                + cw_ref[0:1, lo:lo + cw] * hc_ref[0:ROWS, lo:lo + cw]
                + cb_ref[:, lo:lo + cw])
        hc_ref[0:CONV_HIST, lo:lo + cw] = hc[ROWS - CONV_HIST:ROWS]
        y_ref[:, lo:lo + cw] = (bg * conv * jax.nn.silu(z)).astype(BF16)
    o_ref[...] = h + _dot(y_ref[...], wout_ref[...])


def _window_sum(ref, lo, width, w):
    a = ref[HIST - (w - 1) * BATCH:HIST + ROWS, lo:lo + width]
    span = 1
    while span < w:
        sh = span * BATCH
        a = a[sh:] + a[:-sh]
        span *= 2
    return a


def _pool_kernel(h_ref, g_ref, win_ref, wgrp_ref, bgrp_ref, scale_ref, wout_ref, o_ref, ub_ref, y_ref):
    i = pl.program_id(0)

    @pl.when(i == 0)
    def _():
        ub_ref[0:HIST, :] = jnp.zeros((HIST, POOL_E), F32)

    h = h_ref[...]
    nb = _rms(h, g_ref[...]).astype(BF16)
    pos = i * TQ - PAD + 1 + (lax.broadcasted_iota(jnp.int32, (ROWS, POOL_GROUP), 0) // BATCH)
    for k, w in enumerate(POOL_WINDOWS):
        lo = k * POOL_GROUP
        u = _dot(nb, win_ref[:, lo:lo + POOL_GROUP])
        z = _dot(nb, win_ref[:, POOL_E + lo:POOL_E + lo + POOL_GROUP])
        ub_ref[HIST:HIST + ROWS, lo:lo + POOL_GROUP] = u
        ws = _window_sum(ub_ref, lo, POOL_GROUP, w)
        ub_ref[0:HIST, lo:lo + POOL_GROUP] = u[ROWS - HIST:ROWS]
        cnt = jnp.clip(pos, 1, w).astype(F32)
        mixed = ws / cnt - u
        o = _dot(mixed.astype(BF16), wgrp_ref[k]) + bgrp_ref[k:k + 1, :]
        y = o * scale_ref[:, lo:lo + POOL_GROUP]
        y_ref[:, lo:lo + POOL_GROUP] = (y * jax.nn.silu(z)).astype(BF16)
    o_ref[...] = h + _dot(y_ref[...], wout_ref[...])


def _const_spec(shape):
    nd = len(shape)
    return pl.BlockSpec(shape, lambda i: (0,) * nd, pipeline_mode=pl.Buffered(1))


_TIME_MAJOR = pl.BlockSpec((ROWS, D_MODEL), lambda i: (i, 0))
_BATCH_MAJOR = pl.BlockSpec((BATCH, TQ, D_MODEL), lambda i: (0, jnp.maximum(i - 1, 0), 0))


def _call(body, name, h, consts, scratch, *, head=None, batch_major_out=False):
    pre = [] if head is None else [head]
    out_shape = (BATCH, SEQ, D_MODEL) if batch_major_out else (L_PAD * BATCH, D_MODEL)
    return pl.pallas_call(
        body,
        out_shape=jax.ShapeDtypeStruct(out_shape, h.dtype),
        grid=(N_CHUNKS,),
        in_specs=([_const_spec(p.shape) for p in pre] + [_BATCH_MAJOR if pre else _TIME_MAJOR]
                  + [_const_spec(c.shape) for c in consts]),
        out_specs=_BATCH_MAJOR if batch_major_out else _TIME_MAJOR,
        scratch_shapes=scratch,
        compiler_params=pltpu.CompilerParams(
            dimension_semantics=("arbitrary",), vmem_limit_bytes=VMEM_LIMIT),
        name=name,
    )(*pre, h, *consts)


def _row(v):
    return v.astype(F32).reshape(1, -1)


def _s5_weights(lam_re, lam_im, log_dt, b_re, b_im, c_re, c_im):
    lr = lam_re.astype(F32)
    li = lam_im.astype(F32)
    dt = jnp.exp(log_dt.astype(F32))[:, None]
    mag = jnp.exp(lr * dt)
    ar = mag * jnp.cos(li * dt)
    ai = mag * jnp.sin(li * dt)
    den = lr * lr + li * li
    kr = ((ar - 1.0) * lr + ai * li) / den
    ki = (ai * lr - (ar - 1.0) * li) / den
    br = b_re.astype(F32)
    bi = b_im.astype(F32)
    bbr = kr[..., None] * br - ki[..., None] * bi
    bbi = kr[..., None] * bi + ki[..., None] * br
    gpb = S5_GROUPS // S5_BLOCKS
    eye = jnp.eye(gpb, dtype=F32)

    def bmat(bb):
        t = jnp.einsum("bgpi,gh->bgihp", bb.reshape(S5_BLOCKS, gpb, S5_STATE, S5_GROUP), eye)
        return t.reshape(S5_BLOCKS, S5_BLOCK_CH, S5_BLOCK_ST)

    def cmat(cc):
        t = jnp.einsum("bgip,gh->bgphi", cc.reshape(S5_BLOCKS, gpb, S5_GROUP, S5_STATE), eye)
        return t.reshape(S5_BLOCKS, S5_BLOCK_ST, S5_BLOCK_CH)

    bw = jnp.concatenate([bmat(bbr), bmat(bbi)], axis=2).astype(BF16)
    cw = jnp.concatenate([cmat(c_re.astype(F32)), -cmat(c_im.astype(F32))], axis=1).astype(BF16)
    return bw, cw, ar.reshape(S5_BLOCKS, S5_BLOCK_ST), ai.reshape(S5_BLOCKS, S5_BLOCK_ST)


def _s5_layer(h, g, w_in, lam_re, lam_im, log_dt, b_re, b_im, c_re, c_im, d_skip, w_glu, b_glu, w_out,
              name, *, head=None, final_g=None):
    bw, cw, ar, ai = _s5_weights(lam_re, lam_im, log_dt, b_re, b_im, c_re, c_im)
    first = head is not None
    final = final_g is not None
    fg = _row(final_g) if final else jnp.ones((1, D_MODEL), F32)
    consts = [_row(g), w_in.astype(BF16), bw, cw, ar, ai, _row(d_skip), w_glu.astype(BF16), _row(b_glu),
              w_out.astype(BF16), fg]
    scratch = [pltpu.VMEM((2, ROWS, 2 * S5_BLOCK_ST), F32),
               pltpu.VMEM((S5_BLOCKS, 2, BATCH, S5_BLOCK_ST), F32)]
    return _call(functools.partial(_s5_kernel, first=first, final=final), name, h, consts, scratch, head=head,
                 batch_major_out=final)


def _conv_layer(h, g, w_in, conv_w, conv_b, w_out, name):
    consts = [_row(g), w_in.astype(BF16), conv_w.astype(F32), _row(conv_b), w_out.astype(BF16)]
    scratch = [pltpu.VMEM((CONV_HIST + ROWS, CONV_E), F32), pltpu.VMEM((ROWS, CONV_E), BF16)]
    return _call(_conv_kernel, name, h, consts, scratch)


def _pool_layer(h, g, w_in, w_grp, b_grp, scale, w_out, name):
    consts = [_row(g), w_in.astype(BF16), w_grp.astype(BF16), b_grp.astype(F32), _row(scale),
              w_out.astype(BF16)]
    scratch = [pltpu.VMEM((HIST + ROWS, POOL_E), F32), pltpu.VMEM((ROWS, POOL_E), BF16)]
    return _call(_pool_kernel, name, h, consts, scratch)


def kernel(x, meta_tokens, norm0_g, l0_w_in, l0_lam_re, l0_lam_im, l0_log_dt, l0_b_re, l0_b_im, l0_c_re, l0_c_im, l0_d_skip, l0_w_glu, l0_b_glu, l0_w_out, norm1_g, l1_w_in, l1_conv_w, l1_conv_b, l1_w_out, norm2_g, l2_w_in, l2_w_grp, l2_b_grp, l2_scale, l2_w_out, norm3_g, l3_w_in, l3_lam_re, l3_lam_im, l3_log_dt, l3_b_re, l3_b_im, l3_c_re, l3_c_im, l3_d_skip, l3_w_glu, l3_b_glu, l3_w_out, final_g):
    assert x.shape == (BATCH, SEQ, D_MODEL) and meta_tokens.shape == (N_META, D_MODEL)
    meta = jnp.broadcast_to(meta_tokens.astype(x.dtype)[:, None, :], (N_META, BATCH, D_MODEL))
    head = jnp.concatenate([jnp.zeros((PAD, BATCH, D_MODEL), x.dtype), meta], axis=0).reshape(ROWS, D_MODEL)
    h = _s5_layer(x, norm0_g, l0_w_in, l0_lam_re, l0_lam_im, l0_log_dt, l0_b_re, l0_b_im, l0_c_re, l0_c_im,
                  l0_d_skip, l0_w_glu, l0_b_glu, l0_w_out, "s5_layer0", head=head)
    h = _conv_layer(h, norm1_g, l1_w_in, l1_conv_w, l1_conv_b, l1_w_out, "conv_layer1")
    h = _pool_layer(h, norm2_g, l2_w_in, l2_w_grp, l2_b_grp, l2_scale, l2_w_out, "pool_layer2")
    return _s5_layer(h, norm3_g, l3_w_in, l3_lam_re, l3_lam_im, l3_log_dt, l3_b_re, l3_b_im, l3_c_re, l3_c_im,
                     l3_d_skip, l3_w_glu, l3_b_glu, l3_w_out, "s5_layer3", final_g=final_g)
```

```python
import functools

import jax
import jax.numpy as jnp
from jax import lax
from jax.experimental import pallas as pl
from jax.experimental.pallas import tpu as pltpu

D_MODEL = 1024
BATCH = 8
SEQ = 4096
N_META = 16
EPS = 1e-6
S5_GROUP = 16
S5_GROUPS = 64
S5_STATE = 64
CONV_E = 2048
POOL_E = 2048
POOL_WINDOWS = (2, 4, 8, 16)
POOL_GROUP = 512

TQ = 64
ROWS = TQ * BATCH
PAD = TQ - N_META
L_PAD = SEQ + TQ
N_CHUNKS = L_PAD // TQ
LANES = 128
S5_WIN = 4
WIN_ROWS = ROWS // S5_WIN
QUAD_GROUPS = 4
QUADS = S5_GROUPS // QUAD_GROUPS
QUAD_CH = QUAD_GROUPS * S5_GROUP
QUAD_ST = QUAD_GROUPS * S5_STATE
QUAD_IN = S5_WIN * QUAD_CH
assert QUAD_CH * 2 == LANES and S5_WIN == 4
HIST = 16 * BATCH
CONV_HIST = 2 * BATCH
VMEM_LIMIT = 56 * 1024 * 1024

F32 = jnp.float32
BF16 = jnp.bfloat16


def _rms(h, g):
    ms = jnp.mean(h * h, axis=-1, keepdims=True)
    return h * lax.rsqrt(ms + EPS) * g


def _dot(a, b):
    return jnp.dot(a, b, preferred_element_type=F32)


def _swap_halves(v):
    return pltpu.roll(v, LANES // 2, 1)


def _s5_kernel(*refs, first, final):
    if first:
        head_ref, refs = refs[0], refs[1:]
    (h_ref, g_ref, win_ref, wb_ref, wct_ref, awr_ref, awi_ref, dsk_ref,
     wglu_ref, bglu_ref, wout_ref, fg_ref, o_ref, uw_ref, xs_ref, carry_ref) = refs
    i = pl.program_id(0)

    @pl.when(i == 0)
    def _():
        carry_ref[...] = jnp.zeros_like(carry_ref)

    if first:
        xt = jnp.swapaxes(h_ref[...], 0, 1).reshape(ROWS, D_MODEL)
        h = jnp.where(i == 0, head_ref[...], xt)
    else:
        h = h_ref[...]
    nb = _rms(h, g_ref[...]).astype(BF16)
    uz = _dot(nb, win_ref[...])
    u = uz[:, :D_MODEL]
    z = uz[:, D_MODEL:]

    low = lax.broadcasted_iota(jnp.int32, (WIN_ROWS, LANES), 1) < QUAD_CH
    n_win = TQ // S5_WIN
    ut = u.reshape(n_win, S5_WIN, BATCH, D_MODEL)
    for jj in range(S5_WIN // 2):
        ta = ut[:, 2 * jj].reshape(WIN_ROWS, D_MODEL)
        tb = ut[:, 2 * jj + 1].reshape(WIN_ROWS, D_MODEL)
        for col in range(D_MODEL // LANES):
            a = ta[:, col * LANES:(col + 1) * LANES]
            b = tb[:, col * LANES:(col + 1) * LANES]
            even = jnp.where(low, a, _swap_halves(b))
            odd = jnp.where(low, _swap_halves(a), b)
            uw_ref[:, 2 * col * QUAD_IN + jj * LANES:2 * col * QUAD_IN + (jj + 1) * LANES] = even.astype(BF16)
            uw_ref[:, (2 * col + 1) * QUAD_IN + jj * LANES:(2 * col + 1) * QUAD_IN + (jj + 1) * LANES] = (
                odd.astype(BF16))

    yw = []
    for q in range(QUADS):
        st = 2 * q * QUAD_ST
        uq = uw_ref[:, q * QUAD_IN:(q + 1) * QUAD_IN]
        xs_ref[:, st:st + 2 * QUAD_ST] = _dot(uq, wb_ref[q])
        awr = jnp.broadcast_to(awr_ref[q:q + 1, :], (BATCH, QUAD_ST))
        awi = jnp.broadcast_to(awi_ref[q:q + 1, :], (BATCH, QUAD_ST))
        sr = carry_ref[0, :, q * QUAD_ST:(q + 1) * QUAD_ST]
        si = carry_ref[1, :, q * QUAD_ST:(q + 1) * QUAD_ST]
        for m in range(n_win):
            r = m * BATCH
            xr = xs_ref[r:r + BATCH, st:st + QUAD_ST]
            xi = xs_ref[r:r + BATCH, st + QUAD_ST:st + 2 * QUAD_ST]
            xs_ref[r:r + BATCH, st:st + QUAD_ST] = sr
            xs_ref[r:r + BATCH, st + QUAD_ST:st + 2 * QUAD_ST] = si
            sr, si = awr * sr - awi * si + xr, awr * si + awi * sr + xi
        carry_ref[0, :, q * QUAD_ST:(q + 1) * QUAD_ST] = sr
        carry_ref[1, :, q * QUAD_ST:(q + 1) * QUAD_ST] = si
        lhs = jnp.concatenate([xs_ref[:, st:st + 2 * QUAD_ST].astype(BF16), uq], axis=1)
        yw.append(_dot(lhs, wct_ref[q]))

    cols = [[None] * (D_MODEL // LANES) for _ in range(S5_WIN)]
    for col in range(D_MODEL // LANES):
        for jj in range(S5_WIN // 2):
            p = yw[2 * col][:, jj * LANES:(jj + 1) * LANES]
            q = yw[2 * col + 1][:, jj * LANES:(jj + 1) * LANES]
            cols[2 * jj][col] = jnp.where(low, p, _swap_halves(q))
            cols[2 * jj + 1][col] = jnp.where(low, _swap_halves(p), q)
    yt = [jnp.concatenate(c, axis=1).reshape(n_win, BATCH, D_MODEL) for c in cols]
    y = jnp.stack(yt, axis=1).reshape(ROWS, D_MODEL) + dsk_ref[...] * u
    y = jax.nn.gelu(y)
    y = y * jax.nn.sigmoid(_dot(y.astype(BF16), wglu_ref[...]) + bglu_ref[...])
    y = y * jax.nn.silu(z)
    hn = h + _dot(y.astype(BF16), wout_ref[...])
    if final:
        hn = _rms(hn, fg_ref[...])
        o_ref[...] = jnp.swapaxes(hn.reshape(TQ, BATCH, D_MODEL), 0, 1)
    else:
        o_ref[...] = hn


def _conv_kernel(h_ref, g_ref, win_ref, cw_ref, cb_ref, wout_ref, o_ref, hc_ref, y_ref):
    @pl.when(pl.program_id(0) == 0)
    def _():
        hc_ref[0:CONV_HIST, :] = jnp.zeros((CONV_HIST, CONV_E), F32)

    h = h_ref[...]
    nb = _rms(h, g_ref[...]).astype(BF16)
    cw = 512
    for c in range(CONV_E // cw):
        lo = c * cw
        bg = _dot(nb, win_ref[:, lo:lo + cw])
        cg = _dot(nb, win_ref[:, CONV_E + lo:CONV_E + lo + cw])
        v = _dot(nb, win_ref[:, 2 * CONV_E + lo:2 * CONV_E + lo + cw])
        z = _dot(nb, win_ref[:, 3 * CONV_E + lo:3 * CONV_E + lo + cw])
        hc = cg * v
        hc_ref[CONV_HIST:CONV_HIST + ROWS, lo:lo + cw] = hc
        conv = (cw_ref[2:3, lo:lo + cw] * hc
                + cw_ref[1:2, lo:lo + cw] * hc_ref[BATCH:BATCH + ROWS, lo:lo + cw]
                + cw_ref[0:1, lo:lo + cw] * hc_ref[0:ROWS, lo:lo + cw]
                + cb_ref[:, lo:lo + cw])
        hc_ref[0:CONV_HIST, lo:lo + cw] = hc[ROWS - CONV_HIST:ROWS]
        y_ref[:, lo:lo + cw] = (bg * conv * jax.nn.silu(z)).astype(BF16)
    o_ref[...] = h + _dot(y_ref[...], wout_ref[...])


def _window_sum(ref, lo, width, w):
    a = ref[HIST - (w - 1) * BATCH:HIST + ROWS, lo:lo + width]
    span = 1
    while span < w:
        sh = span * BATCH
        a = a[sh:] + a[:-sh]
        span *= 2
    return a


def _pool_kernel(h_ref, g_ref, win_ref, wgrp_ref, bgrp_ref, scale_ref, wout_ref, o_ref, ub_ref, y_ref):
    i = pl.program_id(0)

    @pl.when(i == 0)
    def _():
        ub_ref[0:HIST, :] = jnp.zeros((HIST, POOL_E), F32)

    h = h_ref[...]
    nb = _rms(h, g_ref[...]).astype(BF16)
    pos = i * TQ - PAD + 1 + (lax.broadcasted_iota(jnp.int32, (ROWS, POOL_GROUP), 0) // BATCH)
    for k, w in enumerate(POOL_WINDOWS):
        lo = k * POOL_GROUP
        u = _dot(nb, win_ref[:, lo:lo + POOL_GROUP])
        z = _dot(nb, win_ref[:, POOL_E + lo:POOL_E + lo + POOL_GROUP])
        ub_ref[HIST:HIST + ROWS, lo:lo + POOL_GROUP] = u
        ws = _window_sum(ub_ref, lo, POOL_GROUP, w)
        ub_ref[0:HIST, lo:lo + POOL_GROUP] = u[ROWS - HIST:ROWS]
        cnt = jnp.clip(pos, 1, w).astype(F32)
        mixed = ws / cnt - u
        o = _dot(mixed.astype(BF16), wgrp_ref[k]) + bgrp_ref[k:k + 1, :]
        y = o * scale_ref[:, lo:lo + POOL_GROUP]
        y_ref[:, lo:lo + POOL_GROUP] = (y * jax.nn.silu(z)).astype(BF16)
    o_ref[...] = h + _dot(y_ref[...], wout_ref[...])


def _const_spec(shape):
    nd = len(shape)
    return pl.BlockSpec(shape, lambda i: (0,) * nd, pipeline_mode=pl.Buffered(1))


_TIME_MAJOR = pl.BlockSpec((ROWS, D_MODEL), lambda i: (i, 0))
_BATCH_MAJOR = pl.BlockSpec((BATCH, TQ, D_MODEL), lambda i: (0, jnp.maximum(i - 1, 0), 0))


def _call(body, name, h, consts, scratch, *, head=None, batch_major_out=False):
    pre = [] if head is None else [head]
    out_shape = (BATCH, SEQ, D_MODEL) if batch_major_out else (L_PAD * BATCH, D_MODEL)
    return pl.pallas_call(
        body,
        out_shape=jax.ShapeDtypeStruct(out_shape, h.dtype),
        grid=(N_CHUNKS,),
        in_specs=([_const_spec(p.shape) for p in pre] + [_BATCH_MAJOR if pre else _TIME_MAJOR]
                  + [_const_spec(c.shape) for c in consts]),
        out_specs=_BATCH_MAJOR if batch_major_out else _TIME_MAJOR,
        scratch_shapes=scratch,
        compiler_params=pltpu.CompilerParams(
            dimension_semantics=("arbitrary",), vmem_limit_bytes=VMEM_LIMIT),
        name=name,
    )(*pre, h, *consts)


def _row(v):
    return v.astype(F32).reshape(1, -1)


def _s5_weights(lam_re, lam_im, log_dt, b_re, b_im, c_re, c_im):
    lr = lam_re.astype(F32)
    li = lam_im.astype(F32)
    dt = jnp.exp(log_dt.astype(F32))[:, None]
    mag = jnp.exp(lr * dt)
    ar = mag * jnp.cos(li * dt)
    ai = mag * jnp.sin(li * dt)
    den = lr * lr + li * li
    kr = ((ar - 1.0) * lr + ai * li) / den
    ki = (ai * lr - (ar - 1.0) * li) / den
    br = b_re.astype(F32)
    bi = b_im.astype(F32)
    bbr = kr[..., None] * br - ki[..., None] * bi
    bbi = kr[..., None] * bi + ki[..., None] * br
    cr = c_re.astype(F32)
    ci = c_im.astype(F32)
    pr = [jnp.ones_like(ar)]
    pi = [jnp.zeros_like(ar)]
    for _ in range(S5_WIN):
        pr, pi = pr + [pr[-1] * ar - pi[-1] * ai], pi + [pr[-1] * ai + pi[-1] * ar]
    w, nq, ng = S5_WIN, QUADS, QUAD_GROUPS
    eye = jnp.eye(ng, dtype=F32)

    t = jnp.stack([
        jnp.stack([pr[w - 1 - j][..., None] * bbr - pi[w - 1 - j][..., None] * bbi for j in range(w)]),
        jnp.stack([pr[w - 1 - j][..., None] * bbi + pi[w - 1 - j][..., None] * bbr for j in range(w)])])
    t = t.reshape(2, w, nq, ng, S5_STATE, S5_GROUP).transpose(2, 1, 5, 0, 3, 4)
    wb = t[:, :, None] * eye.reshape(1, 1, ng, 1, 1, ng, 1)
    wb = wb.reshape(nq, QUAD_IN, 2 * QUAD_ST)

    v = jnp.stack([
        jnp.stack([cr * pr[r + 1][:, None, :] - ci * pi[r + 1][:, None, :] for r in range(w)]),
        jnp.stack([-(cr * pi[r + 1][:, None, :] + ci * pr[r + 1][:, None, :]) for r in range(w)])])
    v = v.reshape(2, w, nq, ng, S5_GROUP, S5_STATE).transpose(2, 0, 3, 5, 1, 4)
    wc = v[:, :, :, :, :, None] * eye.reshape(1, 1, ng, 1, 1, ng, 1)
    wc = wc.reshape(nq, 2 * QUAD_ST, QUAD_IN)

    hp = lax.Precision.HIGHEST
    taps = [jnp.einsum("gop,gpi->goi", cr * pr[d][:, None, :] - ci * pi[d][:, None, :], bbr, precision=hp)
            - jnp.einsum("gop,gpi->goi", cr * pi[d][:, None, :] + ci * pr[d][:, None, :], bbi, precision=hp)
            for d in range(w)]
    zero = jnp.zeros_like(taps[0])
    k = jnp.stack([jnp.stack([taps[r - j] if r >= j else zero for r in range(w)]) for j in range(w)])
    k = k.reshape(w, w, nq, ng, S5_GROUP, S5_GROUP).transpose(2, 0, 5, 1, 3, 4)
    wt = k[:, :, None] * eye.reshape(1, 1, ng, 1, 1, ng, 1)
    wt = wt.reshape(nq, QUAD_IN, QUAD_IN)

    wct = jnp.concatenate([wc, wt], axis=1)
    return wb.astype(BF16), wct.astype(BF16), pr[w].reshape(nq, QUAD_ST), pi[w].reshape(nq, QUAD_ST)


def _s5_layer(h, g, w_in, lam_re, lam_im, log_dt, b_re, b_im, c_re, c_im, d_skip, w_glu, b_glu, w_out,
              name, *, head=None, final_g=None):
    wb, wct, awr, awi = _s5_weights(lam_re, lam_im, log_dt, b_re, b_im, c_re, c_im)
    first = head is not None
    final = final_g is not None
    fg = _row(final_g) if final else jnp.ones((1, D_MODEL), F32)
    consts = [_row(g), w_in.astype(BF16), wb, wct, awr, awi, _row(d_skip), w_glu.astype(BF16), _row(b_glu),
              w_out.astype(BF16), fg]
    scratch = [pltpu.VMEM((WIN_ROWS, QUADS * QUAD_IN), BF16),
               pltpu.VMEM((WIN_ROWS, 2 * QUADS * QUAD_ST), F32),
               pltpu.VMEM((2, BATCH, QUADS * QUAD_ST), F32)]
    return _call(functools.partial(_s5_kernel, first=first, final=final), name, h, consts, scratch, head=head,
                 batch_major_out=final)


def _conv_layer(h, g, w_in, conv_w, conv_b, w_out, name):
    consts = [_row(g), w_in.astype(BF16), conv_w.astype(F32), _row(conv_b), w_out.astype(BF16)]
    scratch = [pltpu.VMEM((CONV_HIST + ROWS, CONV_E), F32), pltpu.VMEM((ROWS, CONV_E), BF16)]
    return _call(_conv_kernel, name, h, consts, scratch)


def _pool_layer(h, g, w_in, w_grp, b_grp, scale, w_out, name):
    consts = [_row(g), w_in.astype(BF16), w_grp.astype(BF16), b_grp.astype(F32), _row(scale),
              w_out.astype(BF16)]
    scratch = [pltpu.VMEM((HIST + ROWS, POOL_E), F32), pltpu.VMEM((ROWS, POOL_E), BF16)]
    return _call(_pool_kernel, name, h, consts, scratch)


def kernel(x, meta_tokens, norm0_g, l0_w_in, l0_lam_re, l0_lam_im, l0_log_dt, l0_b_re, l0_b_im, l0_c_re, l0_c_im, l0_d_skip, l0_w_glu, l0_b_glu, l0_w_out, norm1_g, l1_w_in, l1_conv_w, l1_conv_b, l1_w_out, norm2_g, l2_w_in, l2_w_grp, l2_b_grp, l2_scale, l2_w_out, norm3_g, l3_w_in, l3_lam_re, l3_lam_im, l3_log_dt, l3_b_re, l3_b_im, l3_c_re, l3_c_im, l3_d_skip, l3_w_glu, l3_b_glu, l3_w_out, final_g):
    assert x.shape == (BATCH, SEQ, D_MODEL) and meta_tokens.shape == (N_META, D_MODEL)
    meta = jnp.broadcast_to(meta_tokens.astype(x.dtype)[:, None, :], (N_META, BATCH, D_MODEL))
    head = jnp.concatenate([jnp.zeros((PAD, BATCH, D_MODEL), x.dtype), meta], axis=0).reshape(ROWS, D_MODEL)
    h = _s5_layer(x, norm0_g, l0_w_in, l0_lam_re, l0_lam_im, l0_log_dt, l0_b_re, l0_b_im, l0_c_re, l0_c_im,
                  l0_d_skip, l0_w_glu, l0_b_glu, l0_w_out, "s5_layer0", head=head)
    h = _conv_layer(h, norm1_g, l1_w_in, l1_conv_w, l1_conv_b, l1_w_out, "conv_layer1")
    h = _pool_layer(h, norm2_g, l2_w_in, l2_w_grp, l2_b_grp, l2_scale, l2_w_out, "pool_layer2")
    return _s5_layer(h, norm3_g, l3_w_in, l3_lam_re, l3_lam_im, l3_log_dt, l3_b_re, l3_b_im, l3_c_re, l3_c_im,
                     l3_d_skip, l3_w_glu, l3_b_glu, l3_w_out, "s5_layer3", final_g=final_g)
```

```python
import functools

import jax
import jax.numpy as jnp
from jax import lax
from jax.experimental import pallas as pl
from jax.experimental.pallas import tpu as pltpu

D_MODEL = 1024
BATCH = 8
SEQ = 4096
N_META = 16
EPS = 1e-6
S5_GROUP = 16
S5_GROUPS = 64
S5_STATE = 64
CONV_E = 2048
POOL_E = 2048
POOL_WINDOWS = (2, 4, 8, 16)
POOL_GROUP = 512

TQ = 64
ROWS = TQ * BATCH
PAD = TQ - N_META
L_PAD = SEQ + TQ
N_CHUNKS = L_PAD // TQ
LANES = 128
S5_WIN = 4
WIN_ROWS = ROWS // S5_WIN
QUAD_GROUPS = 4
QUADS = S5_GROUPS // QUAD_GROUPS
QUAD_CH = QUAD_GROUPS * S5_GROUP
QUAD_ST = QUAD_GROUPS * S5_STATE
QUAD_IN = S5_WIN * QUAD_CH
S5_AHEAD = 3
assert QUAD_CH * 2 == LANES and S5_WIN == 4
HIST = 16 * BATCH
CONV_HIST = 2 * BATCH
VMEM_LIMIT = 56 * 1024 * 1024

F32 = jnp.float32
BF16 = jnp.bfloat16


def _rms(h, g):
    ms = jnp.mean(h * h, axis=-1, keepdims=True)
    return h * lax.rsqrt(ms + EPS) * g


def _dot(a, b):
    return jnp.dot(a, b, preferred_element_type=F32)


def _swap_halves(v):
    return pltpu.roll(v, LANES // 2, 1)


def _s5_kernel(*refs, first, final):
    if first:
        head_ref, refs = refs[0], refs[1:]
    (h_ref, g_ref, win_ref, wb_ref, wct_ref, awr_ref, awi_ref, dsk_ref,
     wglu_ref, bglu_ref, wout_ref, fg_ref, o_ref, uw_ref, xs_ref, carry_ref) = refs
    i = pl.program_id(0)

    @pl.when(i == 0)
    def _():
        carry_ref[...] = jnp.zeros_like(carry_ref)

    if first:
        xt = jnp.swapaxes(h_ref[...], 0, 1).reshape(ROWS, D_MODEL)
        h = jnp.where(i == 0, head_ref[...], xt)
    else:
        h = h_ref[...]
    nb = _rms(h, g_ref[...]).astype(BF16)
    uz = _dot(nb, win_ref[...])
    u = uz[:, :D_MODEL]
    z = uz[:, D_MODEL:]

    low = lax.broadcasted_iota(jnp.int32, (WIN_ROWS, LANES), 1) < QUAD_CH
    n_win = TQ // S5_WIN
    ut = u.reshape(n_win, S5_WIN, BATCH, D_MODEL)
    for jj in range(S5_WIN // 2):
        ta = ut[:, 2 * jj].reshape(WIN_ROWS, D_MODEL)
        tb = ut[:, 2 * jj + 1].reshape(WIN_ROWS, D_MODEL)
        for col in range(D_MODEL // LANES):
            a = ta[:, col * LANES:(col + 1) * LANES]
            b = tb[:, col * LANES:(col + 1) * LANES]
            even = jnp.where(low, a, _swap_halves(b))
            odd = jnp.where(low, _swap_halves(a), b)
            uw_ref[:, 2 * col * QUAD_IN + jj * LANES:2 * col * QUAD_IN + (jj + 1) * LANES] = even.astype(BF16)
            uw_ref[:, (2 * col + 1) * QUAD_IN + jj * LANES:(2 * col + 1) * QUAD_IN + (jj + 1) * LANES] = (
                odd.astype(BF16))

    def window_inputs(q):
        xs_ref[:, 2 * q * QUAD_ST:2 * (q + 1) * QUAD_ST] = _dot(uw_ref[:, q * QUAD_IN:(q + 1) * QUAD_IN], wb_ref[q])

    for q in range(S5_AHEAD):
        window_inputs(q)
    yw = []
    for q in range(QUADS):
        if q + S5_AHEAD < QUADS:
            window_inputs(q + S5_AHEAD)
        st = 2 * q * QUAD_ST
        uq = uw_ref[:, q * QUAD_IN:(q + 1) * QUAD_IN]
        awr = jnp.broadcast_to(awr_ref[q:q + 1, :], (BATCH, QUAD_ST))
        awi = jnp.broadcast_to(awi_ref[q:q + 1, :], (BATCH, QUAD_ST))
        sr = carry_ref[0, :, q * QUAD_ST:(q + 1) * QUAD_ST]
        si = carry_ref[1, :, q * QUAD_ST:(q + 1) * QUAD_ST]
        for m in range(n_win):
            r = m * BATCH
            xr = xs_ref[r:r + BATCH, st:st + QUAD_ST]
            xi = xs_ref[r:r + BATCH, st + QUAD_ST:st + 2 * QUAD_ST]
            xs_ref[r:r + BATCH, st:st + QUAD_ST] = sr
            xs_ref[r:r + BATCH, st + QUAD_ST:st + 2 * QUAD_ST] = si
            sr, si = awr * sr - awi * si + xr, awr * si + awi * sr + xi
        carry_ref[0, :, q * QUAD_ST:(q + 1) * QUAD_ST] = sr
        carry_ref[1, :, q * QUAD_ST:(q + 1) * QUAD_ST] = si
        lhs = jnp.concatenate([xs_ref[:, st:st + 2 * QUAD_ST].astype(BF16), uq], axis=1)
        yw.append(_dot(lhs, wct_ref[q]))

    cols = [[None] * (D_MODEL // LANES) for _ in range(S5_WIN)]
    for col in range(D_MODEL // LANES):
        for jj in range(S5_WIN // 2):
            p = yw[2 * col][:, jj * LANES:(jj + 1) * LANES]
            q = yw[2 * col + 1][:, jj * LANES:(jj + 1) * LANES]
            cols[2 * jj][col] = jnp.where(low, p, _swap_halves(q))
            cols[2 * jj + 1][col] = jnp.where(low, _swap_halves(p), q)
    yt = [jnp.concatenate(c, axis=1).reshape(n_win, BATCH, D_MODEL) for c in cols]
    y = jnp.stack(yt, axis=1).reshape(ROWS, D_MODEL) + dsk_ref[...] * u
    half_t = TQ // 2
    for hf in range(2):
        rows = slice(hf * half_t * BATCH, (hf + 1) * half_t * BATCH)
        yh = jax.nn.gelu(y[rows])
        yh = yh * jax.nn.sigmoid(_dot(yh.astype(BF16), wglu_ref[...]) + bglu_ref[...])
        yh = yh * jax.nn.silu(z[rows])
        hn = h[rows] + _dot(yh.astype(BF16), wout_ref[...])
        if final:
            hn = _rms(hn, fg_ref[...])
            o_ref[:, hf * half_t:(hf + 1) * half_t, :] = jnp.swapaxes(hn.reshape(half_t, BATCH, D_MODEL), 0, 1)
        else:
            o_ref[rows, :] = hn


def _conv_kernel(h_ref, g_ref, win_ref, cw_ref, cb_ref, wout_ref, o_ref, hc_ref, y_ref):
    @pl.when(pl.program_id(0) == 0)
    def _():
        hc_ref[0:CONV_HIST, :] = jnp.zeros((CONV_HIST, CONV_E), F32)

    h = h_ref[...]
    nb = _rms(h, g_ref[...]).astype(BF16)
    cw = 512
    for c in range(CONV_E // cw):
        lo = c * cw
        bg = _dot(nb, win_ref[:, lo:lo + cw])
        cg = _dot(nb, win_ref[:, CONV_E + lo:CONV_E + lo + cw])
        v = _dot(nb, win_ref[:, 2 * CONV_E + lo:2 * CONV_E + lo + cw])
        z = _dot(nb, win_ref[:, 3 * CONV_E + lo:3 * CONV_E + lo + cw])
        hc = cg * v
        hc_ref[CONV_HIST:CONV_HIST + ROWS, lo:lo + cw] = hc
        conv = (cw_ref[2:3, lo:lo + cw] * hc
                + cw_ref[1:2, lo:lo + cw] * hc_ref[BATCH:BATCH + ROWS, lo:lo + cw]
                + cw_ref[0:1, lo:lo + cw] * hc_ref[0:ROWS, lo:lo + cw]
                + cb_ref[:, lo:lo + cw])
        hc_ref[0:CONV_HIST, lo:lo + cw] = hc[ROWS - CONV_HIST:ROWS]
        y_ref[:, lo:lo + cw] = (bg * conv * jax.nn.silu(z)).astype(BF16)
    o_ref[...] = h + _dot(y_ref[...], wout_ref[...])


def _window_sum(ref, lo, width, w):
    a = ref[HIST - (w - 1) * BATCH:HIST + ROWS, lo:lo + width]
    span = 1
    while span < w:
        sh = span * BATCH
        a = a[sh:] + a[:-sh]
        span *= 2
    return a


def _pool_kernel(h_ref, g_ref, win_ref, wgrp_ref, bgrp_ref, scale_ref, wout_ref, o_ref, ub_ref, y_ref):
    i = pl.program_id(0)

    @pl.when(i == 0)
    def _():
        ub_ref[0:HIST, :] = jnp.zeros((HIST, POOL_E), F32)

    h = h_ref[...]
    nb = _rms(h, g_ref[...]).astype(BF16)
    pos = i * TQ - PAD + 1 + (lax.broadcasted_iota(jnp.int32, (ROWS, POOL_GROUP), 0) // BATCH)
    for k, w in enumerate(POOL_WINDOWS):
        lo = k * POOL_GROUP
        u = _dot(nb, win_ref[:, lo:lo + POOL_GROUP])
        z = _dot(nb, win_ref[:, POOL_E + lo:POOL_E + lo + POOL_GROUP])
        ub_ref[HIST:HIST + ROWS, lo:lo + POOL_GROUP] = u
        ws = _window_sum(ub_ref, lo, POOL_GROUP, w)
        ub_ref[0:HIST, lo:lo + POOL_GROUP] = u[ROWS - HIST:ROWS]
        cnt = jnp.clip(pos, 1, w).astype(F32)
        mixed = ws / cnt - u
        o = _dot(mixed.astype(BF16), wgrp_ref[k]) + bgrp_ref[k:k + 1, :]
        y = o * scale_ref[:, lo:lo + POOL_GROUP]
        y_ref[:, lo:lo + POOL_GROUP] = (y * jax.nn.silu(z)).astype(BF16)
    o_ref[...] = h + _dot(y_ref[...], wout_ref[...])


def _const_spec(shape):
    nd = len(shape)
    return pl.BlockSpec(shape, lambda i: (0,) * nd, pipeline_mode=pl.Buffered(1))


_TIME_MAJOR = pl.BlockSpec((ROWS, D_MODEL), lambda i: (i, 0))
_BATCH_MAJOR = pl.BlockSpec((BATCH, TQ, D_MODEL), lambda i: (0, jnp.maximum(i - 1, 0), 0))


def _call(body, name, h, consts, scratch, *, head=None, batch_major_out=False):
    pre = [] if head is None else [head]
    out_shape = (BATCH, SEQ, D_MODEL) if batch_major_out else (L_PAD * BATCH, D_MODEL)
    return pl.pallas_call(
        body,
        out_shape=jax.ShapeDtypeStruct(out_shape, h.dtype),
        grid=(N_CHUNKS,),
        in_specs=([_const_spec(p.shape) for p in pre] + [_BATCH_MAJOR if pre else _TIME_MAJOR]
                  + [_const_spec(c.shape) for c in consts]),
        out_specs=_BATCH_MAJOR if batch_major_out else _TIME_MAJOR,
        scratch_shapes=scratch,
        compiler_params=pltpu.CompilerParams(
            dimension_semantics=("arbitrary",), vmem_limit_bytes=VMEM_LIMIT),
        name=name,
    )(*pre, h, *consts)


def _row(v):
    return v.astype(F32).reshape(1, -1)


def _s5_weights(lam_re, lam_im, log_dt, b_re, b_im, c_re, c_im):
    lr = lam_re.astype(F32)
    li = lam_im.astype(F32)
    dt = jnp.exp(log_dt.astype(F32))[:, None]
    mag = jnp.exp(lr * dt)
    ar = mag * jnp.cos(li * dt)
    ai = mag * jnp.sin(li * dt)
    den = lr * lr + li * li
    kr = ((ar - 1.0) * lr + ai * li) / den
    ki = (ai * lr - (ar - 1.0) * li) / den
    br = b_re.astype(F32)
    bi = b_im.astype(F32)
    bbr = kr[..., None] * br - ki[..., None] * bi
    bbi = kr[..., None] * bi + ki[..., None] * br
    cr = c_re.astype(F32)
    ci = c_im.astype(F32)
    pr = [jnp.ones_like(ar)]
    pi = [jnp.zeros_like(ar)]
    for _ in range(S5_WIN):
        pr, pi = pr + [pr[-1] * ar - pi[-1] * ai], pi + [pr[-1] * ai + pi[-1] * ar]
    w, nq, ng = S5_WIN, QUADS, QUAD_GROUPS

    def lanes(v):
        return v.reshape(nq, 1, QUAD_ST)

    def by_state(v):
        return v.reshape(nq, ng, -1, S5_STATE).transpose(0, 2, 1, 3).reshape(nq, -1, QUAD_ST)

    def block_diag(v, width):
        lane_group = (lax.broadcasted_iota(jnp.int32, (ng, v.shape[-1]), 1) // width) % ng
        keep = (lane_group == lax.broadcasted_iota(jnp.int32, (ng, v.shape[-1]), 0)).astype(v.dtype)
        out = v[:, :, None] * keep[None, None, :, None, :]
        return out.reshape(nq, -1, v.shape[-1])

    br_t = by_state(jnp.swapaxes(bbr, 1, 2))
    bi_t = by_state(jnp.swapaxes(bbi, 1, 2))
    cr_t = by_state(cr)
    ci_t = by_state(ci)
    pr_l = [lanes(v) for v in pr]
    pi_l = [lanes(v) for v in pi]

    wb = jnp.stack([jnp.concatenate([pr_l[w - 1 - j] * br_t - pi_l[w - 1 - j] * bi_t,
                                     pr_l[w - 1 - j] * bi_t + pi_l[w - 1 - j] * br_t], axis=-1)
                    for j in range(w)], axis=1)
    wb = block_diag(wb, S5_STATE)

    wc = jnp.stack([jnp.concatenate([cr_t * pr_l[r + 1] - ci_t * pi_l[r + 1],
                                     -(cr_t * pi_l[r + 1] + ci_t * pr_l[r + 1])], axis=-1)
                    for r in range(w)], axis=1)
    wc = jnp.swapaxes(block_diag(wc, S5_STATE), 1, 2)

    hp = lax.Precision.HIGHEST
    taps = [jnp.einsum("gop,gpi->gio", cr * pr[d][:, None, :] - ci * pi[d][:, None, :], bbr, precision=hp)
            - jnp.einsum("gop,gpi->gio", cr * pi[d][:, None, :] + ci * pr[d][:, None, :], bbi, precision=hp)
            for d in range(w)]
    taps = [v.reshape(nq, ng, S5_GROUP, S5_GROUP).transpose(0, 2, 1, 3).reshape(nq, S5_GROUP, QUAD_CH)
            for v in taps]
    zero = jnp.zeros_like(taps[0])
    wt = jnp.stack([jnp.concatenate([taps[r - j] if r >= j else zero for r in range(w)], axis=-1)
                    for j in range(w)], axis=1)
    wt = block_diag(wt, S5_GROUP)

    wct = jnp.concatenate([wc, wt], axis=1)
    return wb.astype(BF16), wct.astype(BF16), pr[w].reshape(nq, QUAD_ST), pi[w].reshape(nq, QUAD_ST)


def _s5_layer(h, g, w_in, lam_re, lam_im, log_dt, b_re, b_im, c_re, c_im, d_skip, w_glu, b_glu, w_out,
              name, *, head=None, final_g=None):
    wb, wct, awr, awi = _s5_weights(lam_re, lam_im, log_dt, b_re, b_im, c_re, c_im)
    first = head is not None
    final = final_g is not None
    fg = _row(final_g) if final else jnp.ones((1, D_MODEL), F32)
    consts = [_row(g), w_in.astype(BF16), wb, wct, awr, awi, _row(d_skip), w_glu.astype(BF16), _row(b_glu),
              w_out.astype(BF16), fg]
    scratch = [pltpu.VMEM((WIN_ROWS, QUADS * QUAD_IN), BF16),
               pltpu.VMEM((WIN_ROWS, 2 * QUADS * QUAD_ST), F32),
               pltpu.VMEM((2, BATCH, QUADS * QUAD_ST), F32)]
    return _call(functools.partial(_s5_kernel, first=first, final=final), name, h, consts, scratch, head=head,
                 batch_major_out=final)


def _conv_layer(h, g, w_in, conv_w, conv_b, w_out, name):
    consts = [_row(g), w_in.astype(BF16), conv_w.astype(F32), _row(conv_b), w_out.astype(BF16)]
    scratch = [pltpu.VMEM((CONV_HIST + ROWS, CONV_E), F32), pltpu.VMEM((ROWS, CONV_E), BF16)]
    return _call(_conv_kernel, name, h, consts, scratch)


def _pool_layer(h, g, w_in, w_grp, b_grp, scale, w_out, name):
    consts = [_row(g), w_in.astype(BF16), w_grp.astype(BF16), b_grp.astype(F32), _row(scale),
              w_out.astype(BF16)]
    scratch = [pltpu.VMEM((HIST + ROWS, POOL_E), F32), pltpu.VMEM((ROWS, POOL_E), BF16)]
    return _call(_pool_kernel, name, h, consts, scratch)


def kernel(x, meta_tokens, norm0_g, l0_w_in, l0_lam_re, l0_lam_im, l0_log_dt, l0_b_re, l0_b_im, l0_c_re, l0_c_im, l0_d_skip, l0_w_glu, l0_b_glu, l0_w_out, norm1_g, l1_w_in, l1_conv_w, l1_conv_b, l1_w_out, norm2_g, l2_w_in, l2_w_grp, l2_b_grp, l2_scale, l2_w_out, norm3_g, l3_w_in, l3_lam_re, l3_lam_im, l3_log_dt, l3_b_re, l3_b_im, l3_c_re, l3_c_im, l3_d_skip, l3_w_glu, l3_b_glu, l3_w_out, final_g):
    assert x.shape == (BATCH, SEQ, D_MODEL) and meta_tokens.shape == (N_META, D_MODEL)
    meta = jnp.broadcast_to(meta_tokens.astype(x.dtype)[:, None, :], (N_META, BATCH, D_MODEL))
    head = jnp.concatenate([jnp.zeros((PAD, BATCH, D_MODEL), x.dtype), meta], axis=0).reshape(ROWS, D_MODEL)
    h = _s5_layer(x, norm0_g, l0_w_in, l0_lam_re, l0_lam_im, l0_log_dt, l0_b_re, l0_b_im, l0_c_re, l0_c_im,
                  l0_d_skip, l0_w_glu, l0_b_glu, l0_w_out, "s5_layer0", head=head)
    h = _conv_layer(h, norm1_g, l1_w_in, l1_conv_w, l1_conv_b, l1_w_out, "conv_layer1")
    h = _pool_layer(h, norm2_g, l2_w_in, l2_w_grp, l2_b_grp, l2_scale, l2_w_out, "pool_layer2")
    return _s5_layer(h, norm3_g, l3_w_in, l3_lam_re, l3_lam_im, l3_log_dt, l3_b_re, l3_b_im, l3_c_re, l3_c_im,
                     l3_d_skip, l3_w_glu, l3_b_glu, l3_w_out, "s5_layer3", final_g=final_g)
```

```python
import functools

import jax
import jax.numpy as jnp
from jax import lax
from jax.experimental import pallas as pl
from jax.experimental.pallas import tpu as pltpu

D_MODEL = 1024
BATCH = 8
SEQ = 4096
N_META = 16
EPS = 1e-6
S5_GROUP = 16
S5_GROUPS = 64
S5_STATE = 64
CONV_E = 2048
POOL_E = 2048
POOL_WINDOWS = (2, 4, 8, 16)
POOL_GROUP = 512

TQ = 128
ROWS = TQ * BATCH
PAD = TQ - N_META
L_PAD = SEQ + TQ
N_CHUNKS = L_PAD // TQ
LANES = 128
S5_WIN = 4
WIN_ROWS = ROWS // S5_WIN
QUAD_GROUPS = 4
QUADS = S5_GROUPS // QUAD_GROUPS
QUAD_CH = QUAD_GROUPS * S5_GROUP
QUAD_ST = QUAD_GROUPS * S5_STATE
QUAD_IN = S5_WIN * QUAD_CH
S5_AHEAD = 3
assert QUAD_CH * 2 == LANES and S5_WIN == 4
HIST = 16 * BATCH
CONV_HIST = 2 * BATCH
VMEM_LIMIT = 56 * 1024 * 1024

F32 = jnp.float32
BF16 = jnp.bfloat16


def _rms(h, g):
    ms = jnp.mean(h * h, axis=-1, keepdims=True)
    return h * lax.rsqrt(ms + EPS) * g


def _dot(a, b):
    return jnp.dot(a, b, preferred_element_type=F32)


def _swap_halves(v):
    return pltpu.roll(v, LANES // 2, 1)


def _s5_kernel(*refs, first, final):
    if first:
        head_ref, refs = refs[0], refs[1:]
    (h_ref, g_ref, win_ref, wb_ref, wct_ref, awr_ref, awi_ref, dsk_ref,
     wglu_ref, bglu_ref, wout_ref, fg_ref, o_ref, uw_ref, xs_ref, carry_ref) = refs
    i = pl.program_id(0)

    @pl.when(i == 0)
    def _():
        carry_ref[...] = jnp.zeros_like(carry_ref)

    if first:
        xt = jnp.swapaxes(h_ref[...], 0, 1).reshape(ROWS, D_MODEL)
        h = jnp.where(i == 0, head_ref[...], xt)
    else:
        h = h_ref[...]
    nb = _rms(h, g_ref[...]).astype(BF16)
    uz = _dot(nb, win_ref[...])
    u = uz[:, :D_MODEL]
    z = uz[:, D_MODEL:]

    low = lax.broadcasted_iota(jnp.int32, (WIN_ROWS, LANES), 1) < QUAD_CH
    n_win = TQ // S5_WIN
    ut = u.reshape(n_win, S5_WIN, BATCH, D_MODEL)
    for jj in range(S5_WIN // 2):
        ta = ut[:, 2 * jj].reshape(WIN_ROWS, D_MODEL)
        tb = ut[:, 2 * jj + 1].reshape(WIN_ROWS, D_MODEL)
        for col in range(D_MODEL // LANES):
            a = ta[:, col * LANES:(col + 1) * LANES]
            b = tb[:, col * LANES:(col + 1) * LANES]
            even = jnp.where(low, a, _swap_halves(b))
            odd = jnp.where(low, _swap_halves(a), b)
            uw_ref[:, 2 * col * QUAD_IN + jj * LANES:2 * col * QUAD_IN + (jj + 1) * LANES] = even.astype(BF16)
            uw_ref[:, (2 * col + 1) * QUAD_IN + jj * LANES:(2 * col + 1) * QUAD_IN + (jj + 1) * LANES] = (
                odd.astype(BF16))

    def window_inputs(q):
        xs_ref[:, 2 * q * QUAD_ST:2 * (q + 1) * QUAD_ST] = _dot(uw_ref[:, q * QUAD_IN:(q + 1) * QUAD_IN], wb_ref[q])

    for q in range(S5_AHEAD):
        window_inputs(q)
    yw = []
    for q in range(QUADS):
        if q + S5_AHEAD < QUADS:
            window_inputs(q + S5_AHEAD)
        st = 2 * q * QUAD_ST
        uq = uw_ref[:, q * QUAD_IN:(q + 1) * QUAD_IN]
        awr = jnp.broadcast_to(awr_ref[q:q + 1, :], (BATCH, QUAD_ST))
        awi = jnp.broadcast_to(awi_ref[q:q + 1, :], (BATCH, QUAD_ST))
        sr = carry_ref[0, :, q * QUAD_ST:(q + 1) * QUAD_ST]
        si = carry_ref[1, :, q * QUAD_ST:(q + 1) * QUAD_ST]
        for m in range(n_win):
            r = m * BATCH
            xr = xs_ref[r:r + BATCH, st:st + QUAD_ST]
            xi = xs_ref[r:r + BATCH, st + QUAD_ST:st + 2 * QUAD_ST]
            xs_ref[r:r + BATCH, st:st + QUAD_ST] = sr
            xs_ref[r:r + BATCH, st + QUAD_ST:st + 2 * QUAD_ST] = si
            sr, si = awr * sr - awi * si + xr, awr * si + awi * sr + xi
        carry_ref[0, :, q * QUAD_ST:(q + 1) * QUAD_ST] = sr
        carry_ref[1, :, q * QUAD_ST:(q + 1) * QUAD_ST] = si
        lhs = jnp.concatenate([xs_ref[:, st:st + 2 * QUAD_ST].astype(BF16), uq], axis=1)
        yw.append(_dot(lhs, wct_ref[q]))

    cols = [[None] * (D_MODEL // LANES) for _ in range(S5_WIN)]
    for col in range(D_MODEL // LANES):
        for jj in range(S5_WIN // 2):
            p = yw[2 * col][:, jj * LANES:(jj + 1) * LANES]
            q = yw[2 * col + 1][:, jj * LANES:(jj + 1) * LANES]
            cols[2 * jj][col] = jnp.where(low, p, _swap_halves(q))
            cols[2 * jj + 1][col] = jnp.where(low, _swap_halves(p), q)
    yt = [jnp.concatenate(c, axis=1).reshape(n_win, BATCH, D_MODEL) for c in cols]
    y = jnp.stack(yt, axis=1).reshape(ROWS, D_MODEL) + dsk_ref[...] * u
    half_t = TQ // 2
    for hf in range(2):
        rows = slice(hf * half_t * BATCH, (hf + 1) * half_t * BATCH)
        yh = jax.nn.gelu(y[rows])
        yh = yh * jax.nn.sigmoid(_dot(yh.astype(BF16), wglu_ref[...]) + bglu_ref[...])
        yh = yh * jax.nn.silu(z[rows])
        hn = h[rows] + _dot(yh.astype(BF16), wout_ref[...])
        if final:
            hn = _rms(hn, fg_ref[...])
            o_ref[:, hf * half_t:(hf + 1) * half_t, :] = jnp.swapaxes(hn.reshape(half_t, BATCH, D_MODEL), 0, 1)
        else:
            o_ref[rows, :] = hn


def _conv_kernel(h_ref, g_ref, win_ref, cw_ref, cb_ref, wout_ref, o_ref, hc_ref, y_ref):
    @pl.when(pl.program_id(0) == 0)
    def _():
        hc_ref[0:CONV_HIST, :] = jnp.zeros((CONV_HIST, CONV_E), F32)

    h = h_ref[...]
    nb = _rms(h, g_ref[...]).astype(BF16)
    cw = 512
    for c in range(CONV_E // cw):
        lo = c * cw
        bg = _dot(nb, win_ref[:, lo:lo + cw])
        cg = _dot(nb, win_ref[:, CONV_E + lo:CONV_E + lo + cw])
        v = _dot(nb, win_ref[:, 2 * CONV_E + lo:2 * CONV_E + lo + cw])
        z = _dot(nb, win_ref[:, 3 * CONV_E + lo:3 * CONV_E + lo + cw])
        hc = cg * v
        hc_ref[CONV_HIST:CONV_HIST + ROWS, lo:lo + cw] = hc
        conv = (cw_ref[2:3, lo:lo + cw] * hc
                + cw_ref[1:2, lo:lo + cw] * hc_ref[BATCH:BATCH + ROWS, lo:lo + cw]
                + cw_ref[0:1, lo:lo + cw] * hc_ref[0:ROWS, lo:lo + cw]
                + cb_ref[:, lo:lo + cw])
        hc_ref[0:CONV_HIST, lo:lo + cw] = hc[ROWS - CONV_HIST:ROWS]
        y_ref[:, lo:lo + cw] = (bg * conv * jax.nn.silu(z)).astype(BF16)
    o_ref[...] = h + _dot(y_ref[...], wout_ref[...])


def _window_sum(ref, lo, width, w):
    a = ref[HIST - (w - 1) * BATCH:HIST + ROWS, lo:lo + width]
    span = 1
    while span < w:
        sh = span * BATCH
        a = a[sh:] + a[:-sh]
        span *= 2
    return a


def _pool_kernel(h_ref, g_ref, win_ref, wgrp_ref, bgrp_ref, scale_ref, wout_ref, o_ref, ub_ref, y_ref):
    i = pl.program_id(0)

    @pl.when(i == 0)
    def _():
        ub_ref[0:HIST, :] = jnp.zeros((HIST, POOL_E), F32)

    h = h_ref[...]
    nb = _rms(h, g_ref[...]).astype(BF16)
    pos = i * TQ - PAD + 1 + (lax.broadcasted_iota(jnp.int32, (ROWS, POOL_GROUP), 0) // BATCH)
    for k, w in enumerate(POOL_WINDOWS):
        lo = k * POOL_GROUP
        u = _dot(nb, win_ref[:, lo:lo + POOL_GROUP])
        z = _dot(nb, win_ref[:, POOL_E + lo:POOL_E + lo + POOL_GROUP])
        ub_ref[HIST:HIST + ROWS, lo:lo + POOL_GROUP] = u
        ws = _window_sum(ub_ref, lo, POOL_GROUP, w)
        ub_ref[0:HIST, lo:lo + POOL_GROUP] = u[ROWS - HIST:ROWS]
        cnt = jnp.clip(pos, 1, w).astype(F32)
        mixed = ws / cnt - u
        o = _dot(mixed.astype(BF16), wgrp_ref[k]) + bgrp_ref[k:k + 1, :]
        y = o * scale_ref[:, lo:lo + POOL_GROUP]
        y_ref[:, lo:lo + POOL_GROUP] = (y * jax.nn.silu(z)).astype(BF16)
    o_ref[...] = h + _dot(y_ref[...], wout_ref[...])


def _const_spec(shape):
    nd = len(shape)
    return pl.BlockSpec(shape, lambda i: (0,) * nd, pipeline_mode=pl.Buffered(1))


_TIME_MAJOR = pl.BlockSpec((ROWS, D_MODEL), lambda i: (i, 0))
_BATCH_MAJOR = pl.BlockSpec((BATCH, TQ, D_MODEL), lambda i: (0, jnp.maximum(i - 1, 0), 0))


def _call(body, name, h, consts, scratch, *, head=None, batch_major_out=False):
    pre = [] if head is None else [head]
    out_shape = (BATCH, SEQ, D_MODEL) if batch_major_out else (L_PAD * BATCH, D_MODEL)
    return pl.pallas_call(
        body,
        out_shape=jax.ShapeDtypeStruct(out_shape, h.dtype),
        grid=(N_CHUNKS,),
        in_specs=([_const_spec(p.shape) for p in pre] + [_BATCH_MAJOR if pre else _TIME_MAJOR]
                  + [_const_spec(c.shape) for c in consts]),
        out_specs=_BATCH_MAJOR if batch_major_out else _TIME_MAJOR,
        scratch_shapes=scratch,
        compiler_params=pltpu.CompilerParams(
            dimension_semantics=("arbitrary",), vmem_limit_bytes=VMEM_LIMIT),
        name=name,
    )(*pre, h, *consts)


def _row(v):
    return v.astype(F32).reshape(1, -1)


def _s5_weights(lam_re, lam_im, log_dt, b_re, b_im, c_re, c_im):
    lr = lam_re.astype(F32)
    li = lam_im.astype(F32)
    dt = jnp.exp(log_dt.astype(F32))[:, None]
    mag = jnp.exp(lr * dt)
    ar = mag * jnp.cos(li * dt)
    ai = mag * jnp.sin(li * dt)
    den = lr * lr + li * li
    kr = ((ar - 1.0) * lr + ai * li) / den
    ki = (ai * lr - (ar - 1.0) * li) / den
    br = b_re.astype(F32)
    bi = b_im.astype(F32)
    bbr = kr[..., None] * br - ki[..., None] * bi
    bbi = kr[..., None] * bi + ki[..., None] * br
    cr = c_re.astype(F32)
    ci = c_im.astype(F32)
    pr = [jnp.ones_like(ar)]
    pi = [jnp.zeros_like(ar)]
    for _ in range(S5_WIN):
        pr, pi = pr + [pr[-1] * ar - pi[-1] * ai], pi + [pr[-1] * ai + pi[-1] * ar]
    w, nq, ng = S5_WIN, QUADS, QUAD_GROUPS

    def lanes(v):
        return v.reshape(nq, 1, QUAD_ST)

    def by_state(v):
        return v.reshape(nq, ng, -1, S5_STATE).transpose(0, 2, 1, 3).reshape(nq, -1, QUAD_ST)

    def block_diag(v, width):
        lane_group = (lax.broadcasted_iota(jnp.int32, (ng, v.shape[-1]), 1) // width) % ng
        keep = (lane_group == lax.broadcasted_iota(jnp.int32, (ng, v.shape[-1]), 0)).astype(v.dtype)
        out = v[:, :, None] * keep[None, None, :, None, :]
        return out.reshape(nq, -1, v.shape[-1])

    br_t = by_state(jnp.swapaxes(bbr, 1, 2))
    bi_t = by_state(jnp.swapaxes(bbi, 1, 2))
    cr_t = by_state(cr)
    ci_t = by_state(ci)
    pr_l = [lanes(v) for v in pr]
    pi_l = [lanes(v) for v in pi]

    wb = jnp.stack([jnp.concatenate([pr_l[w - 1 - j] * br_t - pi_l[w - 1 - j] * bi_t,
                                     pr_l[w - 1 - j] * bi_t + pi_l[w - 1 - j] * br_t], axis=-1)
                    for j in range(w)], axis=1)
    wb = block_diag(wb, S5_STATE)

    wc = jnp.stack([jnp.concatenate([cr_t * pr_l[r + 1] - ci_t * pi_l[r + 1],
                                     -(cr_t * pi_l[r + 1] + ci_t * pr_l[r + 1])], axis=-1)
                    for r in range(w)], axis=1)
    wc = jnp.swapaxes(block_diag(wc, S5_STATE), 1, 2)

    hp = lax.Precision.HIGHEST
    taps = [jnp.einsum("gop,gpi->gio", cr * pr[d][:, None, :] - ci * pi[d][:, None, :], bbr, precision=hp)
            - jnp.einsum("gop,gpi->gio", cr * pi[d][:, None, :] + ci * pr[d][:, None, :], bbi, precision=hp)
            for d in range(w)]
    taps = [v.reshape(nq, ng, S5_GROUP, S5_GROUP).transpose(0, 2, 1, 3).reshape(nq, S5_GROUP, QUAD_CH)
            for v in taps]
    zero = jnp.zeros_like(taps[0])
    wt = jnp.stack([jnp.concatenate([taps[r - j] if r >= j else zero for r in range(w)], axis=-1)
                    for j in range(w)], axis=1)
    wt = block_diag(wt, S5_GROUP)

    wct = jnp.concatenate([wc, wt], axis=1)
    return wb.astype(BF16), wct.astype(BF16), pr[w].reshape(nq, QUAD_ST), pi[w].reshape(nq, QUAD_ST)


def _s5_layer(h, g, w_in, lam_re, lam_im, log_dt, b_re, b_im, c_re, c_im, d_skip, w_glu, b_glu, w_out,
              name, *, head=None, final_g=None):
    wb, wct, awr, awi = _s5_weights(lam_re, lam_im, log_dt, b_re, b_im, c_re, c_im)
    first = head is not None
    final = final_g is not None
    fg = _row(final_g) if final else jnp.ones((1, D_MODEL), F32)
    consts = [_row(g), w_in.astype(BF16), wb, wct, awr, awi, _row(d_skip), w_glu.astype(BF16), _row(b_glu),
              w_out.astype(BF16), fg]
    scratch = [pltpu.VMEM((WIN_ROWS, QUADS * QUAD_IN), BF16),
               pltpu.VMEM((WIN_ROWS, 2 * QUADS * QUAD_ST), F32),
               pltpu.VMEM((2, BATCH, QUADS * QUAD_ST), F32)]
    return _call(functools.partial(_s5_kernel, first=first, final=final), name, h, consts, scratch, head=head,
                 batch_major_out=final)


def _conv_layer(h, g, w_in, conv_w, conv_b, w_out, name):
    consts = [_row(g), w_in.astype(BF16), conv_w.astype(F32), _row(conv_b), w_out.astype(BF16)]
    scratch = [pltpu.VMEM((CONV_HIST + ROWS, CONV_E), F32), pltpu.VMEM((ROWS, CONV_E), BF16)]
    return _call(_conv_kernel, name, h, consts, scratch)


def _pool_layer(h, g, w_in, w_grp, b_grp, scale, w_out, name):
    consts = [_row(g), w_in.astype(BF16), w_grp.astype(BF16), b_grp.astype(F32), _row(scale),
              w_out.astype(BF16)]
    scratch = [pltpu.VMEM((HIST + ROWS, POOL_E), F32), pltpu.VMEM((ROWS, POOL_E), BF16)]
    return _call(_pool_kernel, name, h, consts, scratch)


def kernel(x, meta_tokens, norm0_g, l0_w_in, l0_lam_re, l0_lam_im, l0_log_dt, l0_b_re, l0_b_im, l0_c_re, l0_c_im, l0_d_skip, l0_w_glu, l0_b_glu, l0_w_out, norm1_g, l1_w_in, l1_conv_w, l1_conv_b, l1_w_out, norm2_g, l2_w_in, l2_w_grp, l2_b_grp, l2_scale, l2_w_out, norm3_g, l3_w_in, l3_lam_re, l3_lam_im, l3_log_dt, l3_b_re, l3_b_im, l3_c_re, l3_c_im, l3_d_skip, l3_w_glu, l3_b_glu, l3_w_out, final_g):
    assert x.shape == (BATCH, SEQ, D_MODEL) and meta_tokens.shape == (N_META, D_MODEL)
    meta = jnp.broadcast_to(meta_tokens.astype(x.dtype)[:, None, :], (N_META, BATCH, D_MODEL))
    head = jnp.concatenate([jnp.zeros((PAD, BATCH, D_MODEL), x.dtype), meta], axis=0).reshape(ROWS, D_MODEL)
    h = _s5_layer(x, norm0_g, l0_w_in, l0_lam_re, l0_lam_im, l0_log_dt, l0_b_re, l0_b_im, l0_c_re, l0_c_im,
                  l0_d_skip, l0_w_glu, l0_b_glu, l0_w_out, "s5_layer0", head=head)
    h = _conv_layer(h, norm1_g, l1_w_in, l1_conv_w, l1_conv_b, l1_w_out, "conv_layer1")
    h = _pool_layer(h, norm2_g, l2_w_in, l2_w_grp, l2_b_grp, l2_scale, l2_w_out, "pool_layer2")
    return _s5_layer(h, norm3_g, l3_w_in, l3_lam_re, l3_lam_im, l3_log_dt, l3_b_re, l3_b_im, l3_c_re, l3_c_im,
                     l3_d_skip, l3_w_glu, l3_b_glu, l3_w_out, "s5_layer3", final_g=final_g)
```

```python
import functools

import jax
import jax.numpy as jnp
from jax import lax
from jax.experimental import pallas as pl
from jax.experimental.pallas import tpu as pltpu

D_MODEL = 1024
BATCH = 8
SEQ = 4096
N_META = 16
EPS = 1e-6
S5_GROUP = 16
S5_GROUPS = 64
S5_STATE = 64
CONV_E = 2048
POOL_E = 2048
POOL_WINDOWS = (2, 4, 8, 16)
POOL_GROUP = 512

TQ = 128
ROWS = TQ * BATCH
PAD = TQ - N_META
L_PAD = SEQ + TQ
N_CHUNKS = L_PAD // TQ
LANES = 128
S5_WIN = 4
WIN_ROWS = ROWS // S5_WIN
QUAD_GROUPS = 4
QUADS = S5_GROUPS // QUAD_GROUPS
QUAD_CH = QUAD_GROUPS * S5_GROUP
QUAD_ST = QUAD_GROUPS * S5_STATE
QUAD_IN = S5_WIN * QUAD_CH
S5_AHEAD = 3
TAIL_PARTS = 2
assert QUAD_CH * 2 == LANES and S5_WIN == 4
HIST = 16 * BATCH
CONV_HIST = 2 * BATCH
VMEM_LIMIT = 56 * 1024 * 1024

F32 = jnp.float32
BF16 = jnp.bfloat16


def _rms(h, g):
    ms = jnp.mean(h * h, axis=-1, keepdims=True)
    return h * lax.rsqrt(ms + EPS) * g


def _dot(a, b):
    return jnp.dot(a, b, preferred_element_type=F32)


def _swap_halves(v):
    return pltpu.roll(v, LANES // 2, 1)


def _s5_kernel(*refs, first, final):
    if first:
        head_ref, refs = refs[0], refs[1:]
    (h_ref, g_ref, win_ref, wb_ref, wct_ref, awr_ref, awi_ref, dsk_ref,
     wglu_ref, bglu_ref, wout_ref, fg_ref, o_ref, uw_ref, xs_ref, carry_ref) = refs
    i = pl.program_id(0)

    @pl.when(i == 0)
    def _():
        carry_ref[...] = jnp.zeros_like(carry_ref)

    if first:
        xt = jnp.swapaxes(h_ref[...], 0, 1).reshape(ROWS, D_MODEL)
        head = jnp.concatenate([jnp.zeros((PAD * BATCH, D_MODEL), F32), head_ref[...]], axis=0)
        h = jnp.where(i == 0, head, xt)
    else:
        h = h_ref[...]
    nb = _rms(h, g_ref[...]).astype(BF16)
    uz = _dot(nb, win_ref[...])
    u = uz[:, :D_MODEL]
    z = uz[:, D_MODEL:]

    low = lax.broadcasted_iota(jnp.int32, (WIN_ROWS, LANES), 1) < QUAD_CH
    n_win = TQ // S5_WIN
    ut = u.reshape(n_win, S5_WIN, BATCH, D_MODEL)
    for jj in range(S5_WIN // 2):
        ta = ut[:, 2 * jj].reshape(WIN_ROWS, D_MODEL)
        tb = ut[:, 2 * jj + 1].reshape(WIN_ROWS, D_MODEL)
        for col in range(D_MODEL // LANES):
            a = ta[:, col * LANES:(col + 1) * LANES]
            b = tb[:, col * LANES:(col + 1) * LANES]
            even = jnp.where(low, a, _swap_halves(b))
            odd = jnp.where(low, _swap_halves(a), b)
            uw_ref[:, 2 * col * QUAD_IN + jj * LANES:2 * col * QUAD_IN + (jj + 1) * LANES] = even.astype(BF16)
            uw_ref[:, (2 * col + 1) * QUAD_IN + jj * LANES:(2 * col + 1) * QUAD_IN + (jj + 1) * LANES] = (
                odd.astype(BF16))

    def window_inputs(q):
        xs_ref[:, 2 * q * QUAD_ST:2 * (q + 1) * QUAD_ST] = _dot(uw_ref[:, q * QUAD_IN:(q + 1) * QUAD_IN], wb_ref[q])

    for q in range(S5_AHEAD):
        window_inputs(q)
    yw = []
    for q in range(QUADS):
        if q + S5_AHEAD < QUADS:
            window_inputs(q + S5_AHEAD)
        st = 2 * q * QUAD_ST
        uq = uw_ref[:, q * QUAD_IN:(q + 1) * QUAD_IN]
        awr = jnp.broadcast_to(awr_ref[q:q + 1, :], (BATCH, QUAD_ST))
        awi = jnp.broadcast_to(awi_ref[q:q + 1, :], (BATCH, QUAD_ST))
        sr = carry_ref[0, :, q * QUAD_ST:(q + 1) * QUAD_ST]
        si = carry_ref[1, :, q * QUAD_ST:(q + 1) * QUAD_ST]
        for m in range(n_win):
            r = m * BATCH
            xr = xs_ref[r:r + BATCH, st:st + QUAD_ST]
            xi = xs_ref[r:r + BATCH, st + QUAD_ST:st + 2 * QUAD_ST]
            xs_ref[r:r + BATCH, st:st + QUAD_ST] = sr
            xs_ref[r:r + BATCH, st + QUAD_ST:st + 2 * QUAD_ST] = si
            sr, si = awr * sr - awi * si + xr, awr * si + awi * sr + xi
        carry_ref[0, :, q * QUAD_ST:(q + 1) * QUAD_ST] = sr
        carry_ref[1, :, q * QUAD_ST:(q + 1) * QUAD_ST] = si
        lhs = jnp.concatenate([xs_ref[:, st:st + 2 * QUAD_ST].astype(BF16), uq], axis=1)
        yw.append(_dot(lhs, wct_ref[q]))

    cols = [[None] * (D_MODEL // LANES) for _ in range(S5_WIN)]
    for col in range(D_MODEL // LANES):
        for jj in range(S5_WIN // 2):
            p = yw[2 * col][:, jj * LANES:(jj + 1) * LANES]
            q = yw[2 * col + 1][:, jj * LANES:(jj + 1) * LANES]
            cols[2 * jj][col] = jnp.where(low, p, _swap_halves(q))
            cols[2 * jj + 1][col] = jnp.where(low, _swap_halves(p), q)
    yt = [jnp.concatenate(c, axis=1).reshape(n_win, BATCH, D_MODEL) for c in cols]
    y = jnp.stack(yt, axis=1).reshape(ROWS, D_MODEL) + dsk_ref[...] * u
    half_t = TQ // TAIL_PARTS
    halves = [slice(hf * half_t * BATCH, (hf + 1) * half_t * BATCH) for hf in range(TAIL_PARTS)]
    gelus = [jax.nn.gelu(y[rows]) for rows in halves]
    gates = [_dot(yg.astype(BF16), wglu_ref[...]) for yg in gelus]
    for hf, rows in enumerate(halves):
        yh = gelus[hf] * jax.nn.sigmoid(gates[hf] + bglu_ref[...])
        yh = yh * jax.nn.silu(z[rows])
        hn = h[rows] + _dot(yh.astype(BF16), wout_ref[...])
        if final:
            hn = _rms(hn, fg_ref[...])
            o_ref[:, hf * half_t:(hf + 1) * half_t, :] = jnp.swapaxes(hn.reshape(half_t, BATCH, D_MODEL), 0, 1)
        else:
            o_ref[rows, :] = hn


def _conv_kernel(h_ref, g_ref, win_ref, cw_ref, cb_ref, wout_ref, o_ref, hc_ref, y_ref):
    @pl.when(pl.program_id(0) == 0)
    def _():
        hc_ref[0:CONV_HIST, :] = jnp.zeros((CONV_HIST, CONV_E), F32)

    h = h_ref[...]
    nb = _rms(h, g_ref[...]).astype(BF16)
    cw = 512
    for c in range(CONV_E // cw):
        lo = c * cw
        bg = _dot(nb, win_ref[:, lo:lo + cw])
        cg = _dot(nb, win_ref[:, CONV_E + lo:CONV_E + lo + cw])
        v = _dot(nb, win_ref[:, 2 * CONV_E + lo:2 * CONV_E + lo + cw])
        z = _dot(nb, win_ref[:, 3 * CONV_E + lo:3 * CONV_E + lo + cw])
        hc = cg * v
        hc_ref[CONV_HIST:CONV_HIST + ROWS, lo:lo + cw] = hc
        conv = (cw_ref[2:3, lo:lo + cw] * hc
                + cw_ref[1:2, lo:lo + cw] * hc_ref[BATCH:BATCH + ROWS, lo:lo + cw]
                + cw_ref[0:1, lo:lo + cw] * hc_ref[0:ROWS, lo:lo + cw]
                + cb_ref[:, lo:lo + cw])
        hc_ref[0:CONV_HIST, lo:lo + cw] = hc[ROWS - CONV_HIST:ROWS]
        y_ref[:, lo:lo + cw] = (bg * conv * jax.nn.silu(z)).astype(BF16)
    o_ref[...] = h + _dot(y_ref[...], wout_ref[...])


def _window_sum(ref, lo, width, w):
    a = ref[HIST - (w - 1) * BATCH:HIST + ROWS, lo:lo + width]
    span = 1
    while span < w:
        sh = span * BATCH
        a = a[sh:] + a[:-sh]
        span *= 2
    return a


def _pool_kernel(h_ref, g_ref, win_ref, wgrp_ref, bgrp_ref, scale_ref, wout_ref, o_ref, ub_ref, y_ref):
    i = pl.program_id(0)

    @pl.when(i == 0)
    def _():
        ub_ref[0:HIST, :] = jnp.zeros((HIST, POOL_E), F32)

    h = h_ref[...]
    nb = _rms(h, g_ref[...]).astype(BF16)
    pos = i * TQ - PAD + 1 + (lax.broadcasted_iota(jnp.int32, (ROWS, POOL_GROUP), 0) // BATCH)
    for k, w in enumerate(POOL_WINDOWS):
        lo = k * POOL_GROUP
        u = _dot(nb, win_ref[:, lo:lo + POOL_GROUP])
        z = _dot(nb, win_ref[:, POOL_E + lo:POOL_E + lo + POOL_GROUP])
        ub_ref[HIST:HIST + ROWS, lo:lo + POOL_GROUP] = u
        ws = _window_sum(ub_ref, lo, POOL_GROUP, w)
        ub_ref[0:HIST, lo:lo + POOL_GROUP] = u[ROWS - HIST:ROWS]
        cnt = jnp.clip(pos, 1, w).astype(F32)
        mixed = ws / cnt - u
        o = _dot(mixed.astype(BF16), wgrp_ref[k]) + bgrp_ref[k:k + 1, :]
        y = o * scale_ref[:, lo:lo + POOL_GROUP]
        y_ref[:, lo:lo + POOL_GROUP] = (y * jax.nn.silu(z)).astype(BF16)
    o_ref[...] = h + _dot(y_ref[...], wout_ref[...])


def _const_spec(c):
    if isinstance(c, tuple):
        arr, layer = c
        tail = arr.ndim - 1
        return pl.BlockSpec((None,) + arr.shape[1:], lambda i: (layer,) + (0,) * tail, pipeline_mode=pl.Buffered(1))
    nd = c.ndim
    return pl.BlockSpec(c.shape, lambda i: (0,) * nd, pipeline_mode=pl.Buffered(1))


_TIME_MAJOR = pl.BlockSpec((ROWS, D_MODEL), lambda i: (i, 0))
_BATCH_MAJOR = pl.BlockSpec((BATCH, TQ, D_MODEL), lambda i: (0, jnp.maximum(i - 1, 0), 0))


def _call(body, name, h, consts, scratch, *, head=None, batch_major_out=False):
    pre = [] if head is None else [head]
    out_shape = (BATCH, SEQ, D_MODEL) if batch_major_out else (L_PAD * BATCH, D_MODEL)
    return pl.pallas_call(
        body,
        out_shape=jax.ShapeDtypeStruct(out_shape, h.dtype),
        grid=(N_CHUNKS,),
        in_specs=([_const_spec(p) for p in pre] + [_BATCH_MAJOR if pre else _TIME_MAJOR]
                  + [_const_spec(c) for c in consts]),
        out_specs=_BATCH_MAJOR if batch_major_out else _TIME_MAJOR,
        scratch_shapes=scratch,
        compiler_params=pltpu.CompilerParams(
            dimension_semantics=("arbitrary",), vmem_limit_bytes=VMEM_LIMIT),
        name=name,
    )(*pre, h, *[c[0] if isinstance(c, tuple) else c for c in consts])


def _row(v):
    return v.astype(F32).reshape(1, -1)


def _s5_weights(lam_re, lam_im, log_dt, b_re, b_im, c_re, c_im):
    lr = lam_re.astype(F32)
    li = lam_im.astype(F32)
    dt = jnp.exp(log_dt.astype(F32))[:, None]
    mag = jnp.exp(lr * dt)
    ar = mag * jnp.cos(li * dt)
    ai = mag * jnp.sin(li * dt)
    den = lr * lr + li * li
    kr = ((ar - 1.0) * lr + ai * li) / den
    ki = (ai * lr - (ar - 1.0) * li) / den
    br = b_re.astype(F32)
    bi = b_im.astype(F32)
    bbr = kr[..., None] * br - ki[..., None] * bi
    bbi = kr[..., None] * bi + ki[..., None] * br
    cr = c_re.astype(F32)
    ci = c_im.astype(F32)
    pr = [jnp.ones_like(ar)]
    pi = [jnp.zeros_like(ar)]
    for _ in range(S5_WIN):
        pr, pi = pr + [pr[-1] * ar - pi[-1] * ai], pi + [pr[-1] * ai + pi[-1] * ar]
    w, nq, ng = S5_WIN, QUADS, QUAD_GROUPS

    def lanes(v):
        return v.reshape(nq, 1, QUAD_ST)

    def by_state(v):
        return v.reshape(nq, ng, -1, S5_STATE).transpose(0, 2, 1, 3).reshape(nq, -1, QUAD_ST)

    def block_diag(v, width):
        lane_group = (lax.broadcasted_iota(jnp.int32, (ng, v.shape[-1]), 1) // width) % ng
        keep = (lane_group == lax.broadcasted_iota(jnp.int32, (ng, v.shape[-1]), 0)).astype(v.dtype)
        out = v[:, :, None] * keep[None, None, :, None, :]
        return out.reshape(nq, -1, v.shape[-1])

    br_t = by_state(jnp.swapaxes(bbr, 1, 2))
    bi_t = by_state(jnp.swapaxes(bbi, 1, 2))
    cr_t = by_state(cr)
    ci_t = by_state(ci)
    pr_l = [lanes(v) for v in pr]
    pi_l = [lanes(v) for v in pi]

    wb = jnp.stack([jnp.concatenate([pr_l[w - 1 - j] * br_t - pi_l[w - 1 - j] * bi_t,
                                     pr_l[w - 1 - j] * bi_t + pi_l[w - 1 - j] * br_t], axis=-1)
                    for j in range(w)], axis=1)
    wb = block_diag(wb, S5_STATE)

    wc = jnp.stack([jnp.concatenate([cr_t * pr_l[r + 1] - ci_t * pi_l[r + 1],
                                     -(cr_t * pi_l[r + 1] + ci_t * pr_l[r + 1])], axis=-1)
                    for r in range(w)], axis=1)
    wc = jnp.swapaxes(block_diag(wc, S5_STATE), 1, 2)

    hp = lax.Precision.HIGHEST
    taps = [jnp.einsum("gop,gpi->gio", cr * pr[d][:, None, :] - ci * pi[d][:, None, :], bbr, precision=hp)
            - jnp.einsum("gop,gpi->gio", cr * pi[d][:, None, :] + ci * pr[d][:, None, :], bbi, precision=hp)
            for d in range(w)]
    taps = [v.reshape(nq, ng, S5_GROUP, S5_GROUP).transpose(0, 2, 1, 3).reshape(nq, S5_GROUP, QUAD_CH)
            for v in taps]
    zero = jnp.zeros_like(taps[0])
    wt = jnp.stack([jnp.concatenate([taps[r - j] if r >= j else zero for r in range(w)], axis=-1)
                    for j in range(w)], axis=1)
    wt = block_diag(wt, S5_GROUP)

    wct = jnp.concatenate([wc, wt], axis=1)
    return wb.astype(BF16), wct.astype(BF16), pr[w].reshape(nq, QUAD_ST), pi[w].reshape(nq, QUAD_ST)


def _s5_layer(h, g, w_in, window_weights, d_skip, w_glu, b_glu, w_out, name, *, head=None, final_g=None):
    first = head is not None
    final = final_g is not None
    fg = _row(final_g) if final else jnp.ones((1, D_MODEL), F32)
    consts = [_row(g), w_in.astype(BF16), *window_weights, _row(d_skip), w_glu.astype(BF16), _row(b_glu),
              w_out.astype(BF16), fg]
    scratch = [pltpu.VMEM((WIN_ROWS, QUADS * QUAD_IN), BF16),
               pltpu.VMEM((WIN_ROWS, 2 * QUADS * QUAD_ST), F32),
               pltpu.VMEM((2, BATCH, QUADS * QUAD_ST), F32)]
    return _call(functools.partial(_s5_kernel, first=first, final=final), name, h, consts, scratch, head=head,
                 batch_major_out=final)


def _conv_layer(h, g, w_in, conv_w, conv_b, w_out, name):
    consts = [_row(g), w_in.astype(BF16), conv_w.astype(F32), _row(conv_b), w_out.astype(BF16)]
    scratch = [pltpu.VMEM((CONV_HIST + ROWS, CONV_E), F32), pltpu.VMEM((ROWS, CONV_E), BF16)]
    return _call(_conv_kernel, name, h, consts, scratch)


def _pool_layer(h, g, w_in, w_grp, b_grp, scale, w_out, name):
    consts = [_row(g), w_in.astype(BF16), w_grp.astype(BF16), b_grp.astype(F32), _row(scale),
              w_out.astype(BF16)]
    scratch = [pltpu.VMEM((HIST + ROWS, POOL_E), F32), pltpu.VMEM((ROWS, POOL_E), BF16)]
    return _call(_pool_kernel, name, h, consts, scratch)


def kernel(x, meta_tokens, norm0_g, l0_w_in, l0_lam_re, l0_lam_im, l0_log_dt, l0_b_re, l0_b_im, l0_c_re, l0_c_im, l0_d_skip, l0_w_glu, l0_b_glu, l0_w_out, norm1_g, l1_w_in, l1_conv_w, l1_conv_b, l1_w_out, norm2_g, l2_w_in, l2_w_grp, l2_b_grp, l2_scale, l2_w_out, norm3_g, l3_w_in, l3_lam_re, l3_lam_im, l3_log_dt, l3_b_re, l3_b_im, l3_c_re, l3_c_im, l3_d_skip, l3_w_glu, l3_b_glu, l3_w_out, final_g):
    assert x.shape == (BATCH, SEQ, D_MODEL) and meta_tokens.shape == (N_META, D_MODEL)
    head = jnp.broadcast_to(meta_tokens.astype(x.dtype)[:, None, :], (N_META, BATCH, D_MODEL))
    head = head.reshape(N_META * BATCH, D_MODEL)
    s5_params = [(l0_lam_re, l3_lam_re), (l0_lam_im, l3_lam_im), (l0_log_dt, l3_log_dt), (l0_b_re, l3_b_re),
                 (l0_b_im, l3_b_im), (l0_c_re, l3_c_re), (l0_c_im, l3_c_im)]
    stacked = jax.vmap(_s5_weights)(*[jnp.stack(p) for p in s5_params])
    h = _s5_layer(x, norm0_g, l0_w_in, [(w, 0) for w in stacked], l0_d_skip, l0_w_glu, l0_b_glu, l0_w_out,
                  "s5_layer0", head=head)
    h = _conv_layer(h, norm1_g, l1_w_in, l1_conv_w, l1_conv_b, l1_w_out, "conv_layer1")
    h = _pool_layer(h, norm2_g, l2_w_in, l2_w_grp, l2_b_grp, l2_scale, l2_w_out, "pool_layer2")
    return _s5_layer(h, norm3_g, l3_w_in, [(w, 1) for w in stacked], l3_d_skip, l3_w_glu, l3_b_glu, l3_w_out,
                     "s5_layer3", final_g=final_g)
```

```python
import functools

import jax
import jax.numpy as jnp
from jax import lax
from jax.experimental import pallas as pl
from jax.experimental.pallas import tpu as pltpu

D_MODEL = 1024
BATCH = 8
SEQ = 4096
N_META = 16
EPS = 1e-6
S5_GROUP = 16
S5_GROUPS = 64
S5_STATE = 64
CONV_E = 2048
POOL_E = 2048
POOL_WINDOWS = (2, 4, 8, 16)
POOL_GROUP = 512

TQ = 128
ROWS = TQ * BATCH
PAD = TQ - N_META
L_PAD = SEQ + TQ
N_CHUNKS = L_PAD // TQ
LANES = 128
S5_WIN = 4
WIN_ROWS = ROWS // S5_WIN
QUAD_GROUPS = 4
QUADS = S5_GROUPS // QUAD_GROUPS
QUAD_CH = QUAD_GROUPS * S5_GROUP
QUAD_ST = QUAD_GROUPS * S5_STATE
QUAD_IN = S5_WIN * QUAD_CH
S5_AHEAD = 3
TAIL_PARTS = 2
assert QUAD_CH * 2 == LANES and S5_WIN == 4
HIST = 16 * BATCH
CONV_HIST = 2 * BATCH
VMEM_LIMIT = 56 * 1024 * 1024

F32 = jnp.float32
BF16 = jnp.bfloat16


def _rms(h, g):
    ms = jnp.mean(h * h, axis=-1, keepdims=True)
    return h * lax.rsqrt(ms + EPS) * g


def _dot(a, b):
    return jnp.dot(a, b, preferred_element_type=F32)


def _swap_halves(v):
    return pltpu.roll(v, LANES // 2, 1)


def _s5_kernel(*refs, first, final):
    if first:
        head_ref, refs = refs[0], refs[1:]
    (h_ref, g_ref, win_ref, wb_ref, wct_ref, awr_ref, awi_ref, dsk_ref,
     wglu_ref, bglu_ref, wout_ref, fg_ref, o_ref, uw_ref, xs_ref, carry_ref) = refs
    i = pl.program_id(0)

    @pl.when(i == 0)
    def _():
        carry_ref[...] = jnp.zeros_like(carry_ref)

    if first:
        xt = jnp.swapaxes(h_ref[...], 0, 1).reshape(ROWS, D_MODEL)
        head = jnp.concatenate([jnp.zeros((PAD * BATCH, D_MODEL), F32), head_ref[...]], axis=0)
        h = jnp.where(i == 0, head, xt)
    else:
        h = h_ref[...]
    nb = _rms(h, g_ref[...]).astype(BF16)
    uz = _dot(nb, win_ref[...])
    u = uz[:, :D_MODEL]
    z = uz[:, D_MODEL:]

    low = lax.broadcasted_iota(jnp.int32, (WIN_ROWS, LANES), 1) < QUAD_CH
    n_win = TQ // S5_WIN
    ut = u.reshape(n_win, S5_WIN, BATCH, D_MODEL)
    for jj in range(S5_WIN // 2):
        ta = ut[:, 2 * jj].reshape(WIN_ROWS, D_MODEL)
        tb = ut[:, 2 * jj + 1].reshape(WIN_ROWS, D_MODEL)
        for col in range(D_MODEL // LANES):
            a = ta[:, col * LANES:(col + 1) * LANES]
            b = tb[:, col * LANES:(col + 1) * LANES]
            even = jnp.where(low, a, _swap_halves(b))
            odd = jnp.where(low, _swap_halves(a), b)
            uw_ref[:, 2 * col * QUAD_IN + jj * LANES:2 * col * QUAD_IN + (jj + 1) * LANES] = even.astype(BF16)
            uw_ref[:, (2 * col + 1) * QUAD_IN + jj * LANES:(2 * col + 1) * QUAD_IN + (jj + 1) * LANES] = (
                odd.astype(BF16))

    def window_inputs(q):
        xs_ref[:, 2 * q * QUAD_ST:2 * (q + 1) * QUAD_ST] = _dot(uw_ref[:, q * QUAD_IN:(q + 1) * QUAD_IN], wb_ref[q])

    for q in range(S5_AHEAD):
        window_inputs(q)
    yw = []
    for q in range(QUADS):
        if q + S5_AHEAD < QUADS:
            window_inputs(q + S5_AHEAD)
        st = 2 * q * QUAD_ST
        uq = uw_ref[:, q * QUAD_IN:(q + 1) * QUAD_IN]
        awr = jnp.broadcast_to(awr_ref[q:q + 1, :], (BATCH, QUAD_ST))
        awi = jnp.broadcast_to(awi_ref[q:q + 1, :], (BATCH, QUAD_ST))
        sr = carry_ref[0, :, q * QUAD_ST:(q + 1) * QUAD_ST]
        si = carry_ref[1, :, q * QUAD_ST:(q + 1) * QUAD_ST]
        for m in range(n_win):
            r = m * BATCH
            xr = xs_ref[r:r + BATCH, st:st + QUAD_ST]
            xi = xs_ref[r:r + BATCH, st + QUAD_ST:st + 2 * QUAD_ST]
            xs_ref[r:r + BATCH, st:st + QUAD_ST] = sr
            xs_ref[r:r + BATCH, st + QUAD_ST:st + 2 * QUAD_ST] = si
            sr, si = awr * sr - awi * si + xr, awr * si + awi * sr + xi
        carry_ref[0, :, q * QUAD_ST:(q + 1) * QUAD_ST] = sr
        carry_ref[1, :, q * QUAD_ST:(q + 1) * QUAD_ST] = si
        lhs = jnp.concatenate([xs_ref[:, st:st + 2 * QUAD_ST].astype(BF16), uq], axis=1)
        yw.append(_dot(lhs, wct_ref[q]))

    cols = [[None] * (D_MODEL // LANES) for _ in range(S5_WIN)]
    for col in range(D_MODEL // LANES):
        for jj in range(S5_WIN // 2):
            p = yw[2 * col][:, jj * LANES:(jj + 1) * LANES]
            q = yw[2 * col + 1][:, jj * LANES:(jj + 1) * LANES]
            cols[2 * jj][col] = jnp.where(low, p, _swap_halves(q))
            cols[2 * jj + 1][col] = jnp.where(low, _swap_halves(p), q)
    yt = [jnp.concatenate(c, axis=1).reshape(n_win, BATCH, D_MODEL) for c in cols]
    y = jnp.stack(yt, axis=1).reshape(ROWS, D_MODEL) + dsk_ref[...] * u
    half_t = TQ // TAIL_PARTS
    halves = [slice(hf * half_t * BATCH, (hf + 1) * half_t * BATCH) for hf in range(TAIL_PARTS)]
    gelus = [jax.nn.gelu(y[rows]) for rows in halves]
    gates = [_dot(yg.astype(BF16), wglu_ref[...]) for yg in gelus]
    for hf, rows in enumerate(halves):
        yh = gelus[hf] * jax.nn.sigmoid(gates[hf] + bglu_ref[...])
        yh = yh * jax.nn.silu(z[rows])
        hn = h[rows] + _dot(yh.astype(BF16), wout_ref[...])
        if final:
            hn = _rms(hn, fg_ref[...])
            o_ref[:, hf * half_t:(hf + 1) * half_t, :] = jnp.swapaxes(hn.reshape(half_t, BATCH, D_MODEL), 0, 1)
        else:
            o_ref[rows, :] = hn


def _conv_kernel(h_ref, g_ref, win_ref, cw_ref, cb_ref, wout_ref, o_ref, hc_ref, y_ref):
    @pl.when(pl.program_id(0) == 0)
    def _():
        hc_ref[0:CONV_HIST, :] = jnp.zeros((CONV_HIST, CONV_E), F32)

    h = h_ref[...]
    nb = _rms(h, g_ref[...]).astype(BF16)
    cw = 512
    for c in range(CONV_E // cw):
        lo = c * cw
        bg = _dot(nb, win_ref[:, lo:lo + cw])
        cg = _dot(nb, win_ref[:, CONV_E + lo:CONV_E + lo + cw])
        v = _dot(nb, win_ref[:, 2 * CONV_E + lo:2 * CONV_E + lo + cw])
        z = _dot(nb, win_ref[:, 3 * CONV_E + lo:3 * CONV_E + lo + cw])
        hc = cg * v
        hc_ref[CONV_HIST:CONV_HIST + ROWS, lo:lo + cw] = hc
        conv = (cw_ref[2:3, lo:lo + cw] * hc
                + cw_ref[1:2, lo:lo + cw] * hc_ref[BATCH:BATCH + ROWS, lo:lo + cw]
                + cw_ref[0:1, lo:lo + cw] * hc_ref[0:ROWS, lo:lo + cw]
                + cb_ref[:, lo:lo + cw])
        hc_ref[0:CONV_HIST, lo:lo + cw] = hc[ROWS - CONV_HIST:ROWS]
        y_ref[:, lo:lo + cw] = (bg * conv * jax.nn.silu(z)).astype(BF16)
    o_ref[...] = h + _dot(y_ref[...], wout_ref[...])


def _window_sum(ref, lo, width, w):
    a = ref[HIST - (w - 1) * BATCH:HIST + ROWS, lo:lo + width]
    span = 1
    while span < w:
        sh = span * BATCH
        a = a[sh:] + a[:-sh]
        span *= 2
    return a


def _pool_kernel(h_ref, g_ref, win_ref, wgrp_ref, bgrp_ref, scale_ref, wout_ref, o_ref, ub_ref, y_ref):
    i = pl.program_id(0)

    @pl.when(i == 0)
    def _():
        ub_ref[0:HIST, :] = jnp.zeros((HIST, POOL_E), F32)

    h = h_ref[...]
    nb = _rms(h, g_ref[...]).astype(BF16)
    pos = i * TQ - PAD + 1 + (lax.broadcasted_iota(jnp.int32, (ROWS, POOL_GROUP), 0) // BATCH)
    for k, w in enumerate(POOL_WINDOWS):
        lo = k * POOL_GROUP
        u = _dot(nb, win_ref[:, lo:lo + POOL_GROUP])
        z = _dot(nb, win_ref[:, POOL_E + lo:POOL_E + lo + POOL_GROUP])
        ub_ref[HIST:HIST + ROWS, lo:lo + POOL_GROUP] = u
        ws = _window_sum(ub_ref, lo, POOL_GROUP, w)
        ub_ref[0:HIST, lo:lo + POOL_GROUP] = u[ROWS - HIST:ROWS]
        cnt = jnp.clip(pos, 1, w).astype(F32)
        mixed = ws / cnt - u
        o = _dot(mixed.astype(BF16), wgrp_ref[k]) + bgrp_ref[k:k + 1, :]
        y = o * scale_ref[:, lo:lo + POOL_GROUP]
        y_ref[:, lo:lo + POOL_GROUP] = (y * jax.nn.silu(z)).astype(BF16)
    o_ref[...] = h + _dot(y_ref[...], wout_ref[...])


def _const_spec(c):
    if isinstance(c, tuple):
        arr, layer = c
        tail = arr.ndim - 1
        return pl.BlockSpec((None,) + arr.shape[1:], lambda i: (layer,) + (0,) * tail, pipeline_mode=pl.Buffered(1))
    nd = c.ndim
    return pl.BlockSpec(c.shape, lambda i: (0,) * nd, pipeline_mode=pl.Buffered(1))


_TIME_MAJOR = pl.BlockSpec((ROWS, D_MODEL), lambda i: (i, 0))
_BATCH_MAJOR = pl.BlockSpec((BATCH, TQ, D_MODEL), lambda i: (0, jnp.maximum(i - 1, 0), 0))


def _call(body, name, h, consts, scratch, *, head=None, batch_major_out=False):
    pre = [] if head is None else [head]
    out_shape = (BATCH, SEQ, D_MODEL) if batch_major_out else (L_PAD * BATCH, D_MODEL)
    return pl.pallas_call(
        body,
        out_shape=jax.ShapeDtypeStruct(out_shape, h.dtype),
        grid=(N_CHUNKS,),
        in_specs=([_const_spec(p) for p in pre] + [_BATCH_MAJOR if pre else _TIME_MAJOR]
                  + [_const_spec(c) for c in consts]),
        out_specs=_BATCH_MAJOR if batch_major_out else _TIME_MAJOR,
        scratch_shapes=scratch,
        compiler_params=pltpu.CompilerParams(
            dimension_semantics=("arbitrary",), vmem_limit_bytes=VMEM_LIMIT),
        name=name,
    )(*pre, h, *[c[0] if isinstance(c, tuple) else c for c in consts])


def _row(v):
    return v.astype(F32).reshape(1, -1)


def _s5_weights(lam_re, lam_im, log_dt, b_re, b_im, c_re, c_im):
    lr = lam_re.astype(F32)
    li = lam_im.astype(F32)
    dt = jnp.exp(log_dt.astype(F32))[:, None]
    mag = jnp.exp(lr * dt)
    ar = mag * jnp.cos(li * dt)
    ai = mag * jnp.sin(li * dt)
    den = lr * lr + li * li
    kr = ((ar - 1.0) * lr + ai * li) / den
    ki = (ai * lr - (ar - 1.0) * li) / den
    br = b_re.astype(F32)
    bi = b_im.astype(F32)
    bbr = kr[..., None] * br - ki[..., None] * bi
    bbi = kr[..., None] * bi + ki[..., None] * br
    cr = c_re.astype(F32)
    ci = c_im.astype(F32)
    pr = [jnp.ones_like(ar)]
    pi = [jnp.zeros_like(ar)]
    for _ in range(S5_WIN):
        pr, pi = pr + [pr[-1] * ar - pi[-1] * ai], pi + [pr[-1] * ai + pi[-1] * ar]
    w, nq, ng = S5_WIN, QUADS, QUAD_GROUPS

    def lanes(v):
        return v.reshape(nq, 1, QUAD_ST)

    def by_state(v):
        return v.reshape(nq, ng, -1, S5_STATE).transpose(0, 2, 1, 3).reshape(nq, -1, QUAD_ST)

    def block_diag(v, width):
        v = v.astype(BF16)
        lane_group = (lax.broadcasted_iota(jnp.int32, (ng, v.shape[-1]), 1) // width) % ng
        keep = (lane_group == lax.broadcasted_iota(jnp.int32, (ng, v.shape[-1]), 0)).astype(v.dtype)
        out = v[:, :, None] * keep[None, None, :, None, :]
        return out.reshape(nq, -1, v.shape[-1])

    br_t = by_state(jnp.swapaxes(bbr, 1, 2))
    bi_t = by_state(jnp.swapaxes(bbi, 1, 2))
    cr_t = by_state(cr)
    ci_t = by_state(ci)
    pr_l = [lanes(v) for v in pr]
    pi_l = [lanes(v) for v in pi]

    wb = jnp.stack([jnp.concatenate([pr_l[w - 1 - j] * br_t - pi_l[w - 1 - j] * bi_t,
                                     pr_l[w - 1 - j] * bi_t + pi_l[w - 1 - j] * br_t], axis=-1)
                    for j in range(w)], axis=1)
    wb = block_diag(wb, S5_STATE)

    wc = jnp.stack([jnp.concatenate([cr_t * pr_l[r + 1] - ci_t * pi_l[r + 1],
                                     -(cr_t * pi_l[r + 1] + ci_t * pr_l[r + 1])], axis=-1)
                    for r in range(w)], axis=1)
    wc = jnp.swapaxes(block_diag(wc, S5_STATE), 1, 2)

    hp = lax.Precision.HIGHEST
    taps = [jnp.einsum("gop,gpi->gio", cr * pr[d][:, None, :] - ci * pi[d][:, None, :], bbr, precision=hp)
            - jnp.einsum("gop,gpi->gio", cr * pi[d][:, None, :] + ci * pr[d][:, None, :], bbi, precision=hp)
            for d in range(w)]
    taps = [v.reshape(nq, ng, S5_GROUP, S5_GROUP).transpose(0, 2, 1, 3).reshape(nq, S5_GROUP, QUAD_CH)
            for v in taps]
    zero = jnp.zeros_like(taps[0])
    wt = jnp.stack([jnp.concatenate([taps[r - j] if r >= j else zero for r in range(w)], axis=-1)
                    for j in range(w)], axis=1)
    wt = block_diag(wt, S5_GROUP)

    wct = jnp.concatenate([wc, wt], axis=1)
    return wb, wct, pr[w].reshape(nq, QUAD_ST), pi[w].reshape(nq, QUAD_ST)


def _s5_layer(h, g, w_in, window_weights, d_skip, w_glu, b_glu, w_out, name, *, head=None, final_g=None):
    first = head is not None
    final = final_g is not None
    fg = _row(final_g) if final else jnp.ones((1, D_MODEL), F32)
    consts = [_row(g), w_in.astype(BF16), *window_weights, _row(d_skip), w_glu.astype(BF16), _row(b_glu),
              w_out.astype(BF16), fg]
    scratch = [pltpu.VMEM((WIN_ROWS, QUADS * QUAD_IN), BF16),
               pltpu.VMEM((WIN_ROWS, 2 * QUADS * QUAD_ST), F32),
               pltpu.VMEM((2, BATCH, QUADS * QUAD_ST), F32)]
    return _call(functools.partial(_s5_kernel, first=first, final=final), name, h, consts, scratch, head=head,
                 batch_major_out=final)


def _conv_layer(h, g, w_in, conv_w, conv_b, w_out, name):
    consts = [_row(g), w_in.astype(BF16), conv_w.astype(F32), _row(conv_b), w_out.astype(BF16)]
    scratch = [pltpu.VMEM((CONV_HIST + ROWS, CONV_E), F32), pltpu.VMEM((ROWS, CONV_E), BF16)]
    return _call(_conv_kernel, name, h, consts, scratch)


def _pool_layer(h, g, w_in, w_grp, b_grp, scale, w_out, name):
    consts = [_row(g), w_in.astype(BF16), w_grp.astype(BF16), b_grp.astype(F32), _row(scale),
              w_out.astype(BF16)]
    scratch = [pltpu.VMEM((HIST + ROWS, POOL_E), F32), pltpu.VMEM((ROWS, POOL_E), BF16)]
    return _call(_pool_kernel, name, h, consts, scratch)


def kernel(x, meta_tokens, norm0_g, l0_w_in, l0_lam_re, l0_lam_im, l0_log_dt, l0_b_re, l0_b_im, l0_c_re, l0_c_im, l0_d_skip, l0_w_glu, l0_b_glu, l0_w_out, norm1_g, l1_w_in, l1_conv_w, l1_conv_b, l1_w_out, norm2_g, l2_w_in, l2_w_grp, l2_b_grp, l2_scale, l2_w_out, norm3_g, l3_w_in, l3_lam_re, l3_lam_im, l3_log_dt, l3_b_re, l3_b_im, l3_c_re, l3_c_im, l3_d_skip, l3_w_glu, l3_b_glu, l3_w_out, final_g):
    assert x.shape == (BATCH, SEQ, D_MODEL) and meta_tokens.shape == (N_META, D_MODEL)
    head = jnp.broadcast_to(meta_tokens.astype(x.dtype)[:, None, :], (N_META, BATCH, D_MODEL))
    head = head.reshape(N_META * BATCH, D_MODEL)
    s5_params = [(l0_lam_re, l3_lam_re), (l0_lam_im, l3_lam_im), (l0_log_dt, l3_log_dt), (l0_b_re, l3_b_re),
                 (l0_b_im, l3_b_im), (l0_c_re, l3_c_re), (l0_c_im, l3_c_im)]
    stacked = jax.vmap(_s5_weights)(*[jnp.stack(p) for p in s5_params])
    h = _s5_layer(x, norm0_g, l0_w_in, [(w, 0) for w in stacked], l0_d_skip, l0_w_glu, l0_b_glu, l0_w_out,
                  "s5_layer0", head=head)
    h = _conv_layer(h, norm1_g, l1_w_in, l1_conv_w, l1_conv_b, l1_w_out, "conv_layer1")
    h = _pool_layer(h, norm2_g, l2_w_in, l2_w_grp, l2_b_grp, l2_scale, l2_w_out, "pool_layer2")
    return _s5_layer(h, norm3_g, l3_w_in, [(w, 1) for w in stacked], l3_d_skip, l3_w_glu, l3_b_glu, l3_w_out,
                     "s5_layer3", final_g=final_g)
```

```python
import functools

import jax
import jax.numpy as jnp
from jax import lax
from jax.experimental import pallas as pl
from jax.experimental.pallas import tpu as pltpu

D_MODEL = 1024
BATCH = 8
SEQ = 4096
N_META = 16
EPS = 1e-6
S5_GROUP = 16
S5_GROUPS = 64
S5_STATE = 64
CONV_E = 2048
CONV_K = 3
POOL_E = 2048
POOL_WINDOWS = (2, 4, 8, 16)
POOL_GROUP = 512

TQ = 128
ROWS = TQ * BATCH
PAD = TQ - N_META
L_PAD = SEQ + TQ
N_CHUNKS = L_PAD // TQ
LANES = 128
S5_WIN = 4
WIN_ROWS = ROWS // S5_WIN
QUAD_GROUPS = 4
QUADS = S5_GROUPS // QUAD_GROUPS
QUAD_CH = QUAD_GROUPS * S5_GROUP
QUAD_ST = QUAD_GROUPS * S5_STATE
QUAD_IN = S5_WIN * QUAD_CH
S5_AHEAD = 3
TAIL_PARTS = 2
assert QUAD_CH * 2 == LANES and S5_WIN == 4
HIST = max(POOL_WINDOWS) * BATCH
CONV_HIST = (CONV_K - 1) * BATCH
assert CONV_K == 3
CONV_BLOCK = 512
V7X_VMEM_BYTES = 64 * 1024 * 1024
VMEM_LIMIT = V7X_VMEM_BYTES * 7 // 8

F32 = jnp.float32
BF16 = jnp.bfloat16


def _rms(h, g):
    ms = jnp.mean(h * h, axis=-1, keepdims=True)
    return h * lax.rsqrt(ms + EPS) * g


def _dot(a, b):
    return jnp.dot(a, b, preferred_element_type=F32)


def _swap_halves(v):
    return pltpu.roll(v, LANES // 2, 1)


def _s5_kernel(*refs, first, final):
    if first:
        head_ref, refs = refs[0], refs[1:]
    (h_ref, g_ref, win_ref, wb_ref, wc_ref, wt_ref, awr_ref, awi_ref, dsk_ref,
     wglu_ref, bglu_ref, wout_ref, fg_ref, o_ref, uw_ref, xs_ref, carry_ref) = refs
    i = pl.program_id(0)

    @pl.when(i == 0)
    def _():
        carry_ref[...] = jnp.zeros_like(carry_ref)

    if first:
        xt = jnp.swapaxes(h_ref[...], 0, 1).reshape(ROWS, D_MODEL)
        head = jnp.concatenate([jnp.zeros((PAD * BATCH, D_MODEL), F32), head_ref[...]], axis=0)
        h = jnp.where(i == 0, head, xt)
    else:
        h = h_ref[...]
    nb = _rms(h, g_ref[...]).astype(BF16)
    uz = _dot(nb, win_ref[...])
    u = uz[:, :D_MODEL]
    z = uz[:, D_MODEL:]

    low = lax.broadcasted_iota(jnp.int32, (WIN_ROWS, LANES), 1) < QUAD_CH
    n_win = TQ // S5_WIN
    ut = u.reshape(n_win, S5_WIN, BATCH, D_MODEL)
    for jj in range(S5_WIN // 2):
        ta = ut[:, 2 * jj].reshape(WIN_ROWS, D_MODEL)
        tb = ut[:, 2 * jj + 1].reshape(WIN_ROWS, D_MODEL)
        for col in range(D_MODEL // LANES):
            a = ta[:, col * LANES:(col + 1) * LANES]
            b = tb[:, col * LANES:(col + 1) * LANES]
            even = jnp.where(low, a, _swap_halves(b))
            odd = jnp.where(low, _swap_halves(a), b)
            uw_ref[:, 2 * col * QUAD_IN + jj * LANES:2 * col * QUAD_IN + (jj + 1) * LANES] = even.astype(BF16)
            uw_ref[:, (2 * col + 1) * QUAD_IN + jj * LANES:(2 * col + 1) * QUAD_IN + (jj + 1) * LANES] = (
                odd.astype(BF16))

    def window_inputs(q):
        xs_ref[:, 2 * q * QUAD_ST:2 * (q + 1) * QUAD_ST] = _dot(uw_ref[:, q * QUAD_IN:(q + 1) * QUAD_IN], wb_ref[q])

    for q in range(S5_AHEAD):
        window_inputs(q)
    yw = []
    for q in range(QUADS):
        if q + S5_AHEAD < QUADS:
            window_inputs(q + S5_AHEAD)
        st = 2 * q * QUAD_ST
        uq = uw_ref[:, q * QUAD_IN:(q + 1) * QUAD_IN]
        awr = jnp.broadcast_to(awr_ref[q:q + 1, :], (BATCH, QUAD_ST))
        awi = jnp.broadcast_to(awi_ref[q:q + 1, :], (BATCH, QUAD_ST))
        sr = carry_ref[0, :, q * QUAD_ST:(q + 1) * QUAD_ST]
        si = carry_ref[1, :, q * QUAD_ST:(q + 1) * QUAD_ST]
        for m in range(n_win):
            r = m * BATCH
            xr = xs_ref[r:r + BATCH, st:st + QUAD_ST]
            xi = xs_ref[r:r + BATCH, st + QUAD_ST:st + 2 * QUAD_ST]
            xs_ref[r:r + BATCH, st:st + QUAD_ST] = sr
            xs_ref[r:r + BATCH, st + QUAD_ST:st + 2 * QUAD_ST] = si
            sr, si = awr * sr - awi * si + xr, awr * si + awi * sr + xi
        carry_ref[0, :, q * QUAD_ST:(q + 1) * QUAD_ST] = sr
        carry_ref[1, :, q * QUAD_ST:(q + 1) * QUAD_ST] = si
        s_in = xs_ref[:, st:st + 2 * QUAD_ST].astype(BF16)
        yw.append(_dot(s_in, wc_ref[q]) + _dot(uq, wt_ref[q]))

    cols = [[None] * (D_MODEL // LANES) for _ in range(S5_WIN)]
    for col in range(D_MODEL // LANES):
        for jj in range(S5_WIN // 2):
            p = yw[2 * col][:, jj * LANES:(jj + 1) * LANES]
            q = yw[2 * col + 1][:, jj * LANES:(jj + 1) * LANES]
            cols[2 * jj][col] = jnp.where(low, p, _swap_halves(q))
            cols[2 * jj + 1][col] = jnp.where(low, _swap_halves(p), q)
    yt = [jnp.concatenate(c, axis=1).reshape(n_win, BATCH, D_MODEL) for c in cols]
    y = jnp.stack(yt, axis=1).reshape(ROWS, D_MODEL) + dsk_ref[...] * u
    half_t = TQ // TAIL_PARTS
    halves = [slice(hf * half_t * BATCH, (hf + 1) * half_t * BATCH) for hf in range(TAIL_PARTS)]
    gelus = [jax.nn.gelu(y[rows]) for rows in halves]
    gates = [_dot(yg.astype(BF16), wglu_ref[...]) for yg in gelus]
    for hf, rows in enumerate(halves):
        yh = gelus[hf] * jax.nn.sigmoid(gates[hf] + bglu_ref[...])
        yh = yh * jax.nn.silu(z[rows])
        hn = h[rows] + _dot(yh.astype(BF16), wout_ref[...])
        if final:
            hn = _rms(hn, fg_ref[...])
            o_ref[:, hf * half_t:(hf + 1) * half_t, :] = jnp.swapaxes(hn.reshape(half_t, BATCH, D_MODEL), 0, 1)
        else:
            o_ref[rows, :] = hn


def _conv_kernel(h_ref, g_ref, win_ref, cw_ref, cb_ref, wout_ref, o_ref, hc_ref, y_ref):
    @pl.when(pl.program_id(0) == 0)
    def _():
        hc_ref[0:CONV_HIST, :] = jnp.zeros((CONV_HIST, CONV_E), F32)

    h = h_ref[...]
    nb = _rms(h, g_ref[...]).astype(BF16)
    cw = CONV_BLOCK
    for c in range(CONV_E // cw):
        lo = c * cw
        bg = _dot(nb, win_ref[:, lo:lo + cw])
        cg = _dot(nb, win_ref[:, CONV_E + lo:CONV_E + lo + cw])
        v = _dot(nb, win_ref[:, 2 * CONV_E + lo:2 * CONV_E + lo + cw])
        z = _dot(nb, win_ref[:, 3 * CONV_E + lo:3 * CONV_E + lo + cw])
        hc = cg * v
        hc_ref[CONV_HIST:CONV_HIST + ROWS, lo:lo + cw] = hc
        conv = (cw_ref[2:3, lo:lo + cw] * hc
                + cw_ref[1:2, lo:lo + cw] * hc_ref[BATCH:BATCH + ROWS, lo:lo + cw]
                + cw_ref[0:1, lo:lo + cw] * hc_ref[0:ROWS, lo:lo + cw]
                + cb_ref[:, lo:lo + cw])
        hc_ref[0:CONV_HIST, lo:lo + cw] = hc[ROWS - CONV_HIST:ROWS]
        y_ref[:, lo:lo + cw] = (bg * conv * jax.nn.silu(z)).astype(BF16)
    o_ref[...] = h + _dot(y_ref[...], wout_ref[...])


def _window_sum(ref, lo, width, w):
    a = ref[HIST - (w - 1) * BATCH:HIST + ROWS, lo:lo + width]
    span = 1
    while span < w:
        sh = span * BATCH
        a = a[sh:] + a[:-sh]
        span *= 2
    return a


def _pool_kernel(h_ref, g_ref, win_ref, wgrp_ref, bgrp_ref, scale_ref, wout_ref, o_ref, ub_ref, y_ref):
    i = pl.program_id(0)

    @pl.when(i == 0)
    def _():
        ub_ref[0:HIST, :] = jnp.zeros((HIST, POOL_E), F32)

    h = h_ref[...]
    nb = _rms(h, g_ref[...]).astype(BF16)
    pos = i * TQ - PAD + 1 + (lax.broadcasted_iota(jnp.int32, (ROWS, POOL_GROUP), 0) // BATCH)
    for k, w in enumerate(POOL_WINDOWS):
        lo = k * POOL_GROUP
        u = _dot(nb, win_ref[:, lo:lo + POOL_GROUP])
        z = _dot(nb, win_ref[:, POOL_E + lo:POOL_E + lo + POOL_GROUP])
        ub_ref[HIST:HIST + ROWS, lo:lo + POOL_GROUP] = u
        ws = _window_sum(ub_ref, lo, POOL_GROUP, w)
        ub_ref[0:HIST, lo:lo + POOL_GROUP] = u[ROWS - HIST:ROWS]
        cnt = jnp.clip(pos, 1, w).astype(F32)
        mixed = ws / cnt - u
        o = _dot(mixed.astype(BF16), wgrp_ref[k]) + bgrp_ref[k:k + 1, :]
        y = o * scale_ref[:, lo:lo + POOL_GROUP]
        y_ref[:, lo:lo + POOL_GROUP] = (y * jax.nn.silu(z)).astype(BF16)
    o_ref[...] = h + _dot(y_ref[...], wout_ref[...])


def _const_spec(c):
    if isinstance(c, tuple):
        arr, layer = c
        tail = arr.ndim - 1
        return pl.BlockSpec((None,) + arr.shape[1:], lambda i: (layer,) + (0,) * tail, pipeline_mode=pl.Buffered(1))
    nd = c.ndim
    return pl.BlockSpec(c.shape, lambda i: (0,) * nd, pipeline_mode=pl.Buffered(1))


_TIME_MAJOR = pl.BlockSpec((ROWS, D_MODEL), lambda i: (i, 0))
_BATCH_MAJOR = pl.BlockSpec((BATCH, TQ, D_MODEL), lambda i: (0, jnp.maximum(i - 1, 0), 0))


def _call(body, name, h, consts, scratch, *, head=None, batch_major_out=False):
    pre = [] if head is None else [head]
    out_shape = (BATCH, SEQ, D_MODEL) if batch_major_out else (L_PAD * BATCH, D_MODEL)
    return pl.pallas_call(
        body,
        out_shape=jax.ShapeDtypeStruct(out_shape, h.dtype),
        grid=(N_CHUNKS,),
        in_specs=([_const_spec(p) for p in pre] + [_BATCH_MAJOR if pre else _TIME_MAJOR]
                  + [_const_spec(c) for c in consts]),
        out_specs=_BATCH_MAJOR if batch_major_out else _TIME_MAJOR,
        scratch_shapes=scratch,
        compiler_params=pltpu.CompilerParams(
            dimension_semantics=("arbitrary",), vmem_limit_bytes=VMEM_LIMIT),
        name=name,
    )(*pre, h, *[c[0] if isinstance(c, tuple) else c for c in consts])


def _row(v):
    return v.astype(F32).reshape(1, -1)


def _s5_weights(lam_re, lam_im, log_dt, b_re, b_im, c_re, c_im):
    lr = lam_re.astype(F32)
    li = lam_im.astype(F32)
    dt = jnp.exp(log_dt.astype(F32))[:, None]
    mag = jnp.exp(lr * dt)
    ar = mag * jnp.cos(li * dt)
    ai = mag * jnp.sin(li * dt)
    den = lr * lr + li * li
    kr = ((ar - 1.0) * lr + ai * li) / den
    ki = (ai * lr - (ar - 1.0) * li) / den
    br = b_re.astype(F32)
    bi = b_im.astype(F32)
    bbr = kr[..., None] * br - ki[..., None] * bi
    bbi = kr[..., None] * bi + ki[..., None] * br
    cr = c_re.astype(F32)
    ci = c_im.astype(F32)
    pr = [jnp.ones_like(ar)]
    pi = [jnp.zeros_like(ar)]
    for _ in range(S5_WIN):
        pr, pi = pr + [pr[-1] * ar - pi[-1] * ai], pi + [pr[-1] * ai + pi[-1] * ar]
    w, nq, ng = S5_WIN, QUADS, QUAD_GROUPS

    def lanes(v):
        return v.reshape(nq, 1, QUAD_ST)

    def by_state(v):
        return v.reshape(nq, ng, -1, S5_STATE).transpose(0, 2, 1, 3).reshape(nq, -1, QUAD_ST)

    def block_diag(v, width):
        v = v.astype(BF16)
        lane_group = (lax.broadcasted_iota(jnp.int32, (ng, v.shape[-1]), 1) // width) % ng
        keep = (lane_group == lax.broadcasted_iota(jnp.int32, (ng, v.shape[-1]), 0)).astype(v.dtype)
        out = v[:, :, None] * keep[None, None, :, None, :]
        return out.reshape(nq, -1, v.shape[-1])

    br_t = by_state(jnp.swapaxes(bbr, 1, 2))
    bi_t = by_state(jnp.swapaxes(bbi, 1, 2))
    cr_t = by_state(cr)
    ci_t = by_state(ci)
    pr_l = [lanes(v) for v in pr]
    pi_l = [lanes(v) for v in pi]

    wb = jnp.stack([jnp.concatenate([pr_l[w - 1 - j] * br_t - pi_l[w - 1 - j] * bi_t,
                                     pr_l[w - 1 - j] * bi_t + pi_l[w - 1 - j] * br_t], axis=-1)
                    for j in range(w)], axis=1)
    wb = block_diag(wb, S5_STATE)

    wc = jnp.stack([jnp.concatenate([cr_t * pr_l[r + 1] - ci_t * pi_l[r + 1],
                                     -(cr_t * pi_l[r + 1] + ci_t * pr_l[r + 1])], axis=-1)
                    for r in range(w)], axis=1)
    wc = jnp.swapaxes(block_diag(wc, S5_STATE), 1, 2)

    hp = lax.Precision.HIGHEST
    taps = [jnp.einsum("gop,gpi->gio", cr * pr[d][:, None, :] - ci * pi[d][:, None, :], bbr, precision=hp)
            - jnp.einsum("gop,gpi->gio", cr * pi[d][:, None, :] + ci * pr[d][:, None, :], bbi, precision=hp)
            for d in range(w)]
    taps = [v.reshape(nq, ng, S5_GROUP, S5_GROUP).transpose(0, 2, 1, 3).reshape(nq, S5_GROUP, QUAD_CH)
            for v in taps]
    zero = jnp.zeros_like(taps[0])
    wt = jnp.stack([jnp.concatenate([taps[r - j] if r >= j else zero for r in range(w)], axis=-1)
                    for j in range(w)], axis=1)
    wt = block_diag(wt, S5_GROUP)

    return wb, wc, wt, pr[w].reshape(nq, QUAD_ST), pi[w].reshape(nq, QUAD_ST)


def _s5_layer(h, g, w_in, window_weights, d_skip, w_glu, b_glu, w_out, name, *, head=None, final_g=None):
    first = head is not None
    final = final_g is not None
    fg = _row(final_g) if final else jnp.ones((1, D_MODEL), F32)
    consts = [_row(g), w_in.astype(BF16), *window_weights, _row(d_skip), w_glu.astype(BF16), _row(b_glu),
              w_out.astype(BF16), fg]
    scratch = [pltpu.VMEM((WIN_ROWS, QUADS * QUAD_IN), BF16),
               pltpu.VMEM((WIN_ROWS, 2 * QUADS * QUAD_ST), F32),
               pltpu.VMEM((2, BATCH, QUADS * QUAD_ST), F32)]
    return _call(functools.partial(_s5_kernel, first=first, final=final), name, h, consts, scratch, head=head,
                 batch_major_out=final)


def _conv_layer(h, g, w_in, conv_w, conv_b, w_out, name):
    consts = [_row(g), w_in.astype(BF16), conv_w.astype(F32), _row(conv_b), w_out.astype(BF16)]
    scratch = [pltpu.VMEM((CONV_HIST + ROWS, CONV_E), F32), pltpu.VMEM((ROWS, CONV_E), BF16)]
    return _call(_conv_kernel, name, h, consts, scratch)


def _pool_layer(h, g, w_in, w_grp, b_grp, scale, w_out, name):
    consts = [_row(g), w_in.astype(BF16), w_grp.astype(BF16), b_grp.astype(F32), _row(scale),
              w_out.astype(BF16)]
    scratch = [pltpu.VMEM((HIST + ROWS, POOL_E), F32), pltpu.VMEM((ROWS, POOL_E), BF16)]
    return _call(_pool_kernel, name, h, consts, scratch)


def kernel(x, meta_tokens, norm0_g, l0_w_in, l0_lam_re, l0_lam_im, l0_log_dt, l0_b_re, l0_b_im, l0_c_re, l0_c_im, l0_d_skip, l0_w_glu, l0_b_glu, l0_w_out, norm1_g, l1_w_in, l1_conv_w, l1_conv_b, l1_w_out, norm2_g, l2_w_in, l2_w_grp, l2_b_grp, l2_scale, l2_w_out, norm3_g, l3_w_in, l3_lam_re, l3_lam_im, l3_log_dt, l3_b_re, l3_b_im, l3_c_re, l3_c_im, l3_d_skip, l3_w_glu, l3_b_glu, l3_w_out, final_g):
    assert x.shape == (BATCH, SEQ, D_MODEL) and meta_tokens.shape == (N_META, D_MODEL)
    head = jnp.broadcast_to(meta_tokens.astype(x.dtype)[:, None, :], (N_META, BATCH, D_MODEL))
    head = head.reshape(N_META * BATCH, D_MODEL)
    s5_params = [(l0_lam_re, l3_lam_re), (l0_lam_im, l3_lam_im), (l0_log_dt, l3_log_dt), (l0_b_re, l3_b_re),
                 (l0_b_im, l3_b_im), (l0_c_re, l3_c_re), (l0_c_im, l3_c_im)]
    stacked = jax.vmap(_s5_weights)(*[jnp.stack(p) for p in s5_params])
    h = _s5_layer(x, norm0_g, l0_w_in, [(w, 0) for w in stacked], l0_d_skip, l0_w_glu, l0_b_glu, l0_w_out,
                  "s5_layer0", head=head)
    h = _conv_layer(h, norm1_g, l1_w_in, l1_conv_w, l1_conv_b, l1_w_out, "conv_layer1")
    h = _pool_layer(h, norm2_g, l2_w_in, l2_w_grp, l2_b_grp, l2_scale, l2_w_out, "pool_layer2")
    return _s5_layer(h, norm3_g, l3_w_in, [(w, 1) for w in stacked], l3_d_skip, l3_w_glu, l3_b_glu, l3_w_out,
                     "s5_layer3", final_g=final_g)
```

```python
import functools

import jax
import jax.numpy as jnp
from jax import lax
from jax.experimental import pallas as pl
from jax.experimental.pallas import tpu as pltpu

D_MODEL = 1024
BATCH = 8
SEQ = 4096
N_META = 16
EPS = 1e-6
S5_GROUP = 16
S5_GROUPS = 64
S5_STATE = 64
CONV_E = 2048
CONV_K = 3
POOL_E = 2048
POOL_WINDOWS = (2, 4, 8, 16)
POOL_GROUP = 512

TQ = 128
ROWS = TQ * BATCH
PAD = TQ - N_META
L_PAD = SEQ + TQ
N_CHUNKS = L_PAD // TQ
LANES = 128
S5_WIN = 4
WIN_ROWS = ROWS // S5_WIN
QUAD_GROUPS = 4
QUADS = S5_GROUPS // QUAD_GROUPS
QUAD_CH = QUAD_GROUPS * S5_GROUP
QUAD_ST = QUAD_GROUPS * S5_STATE
QUAD_IN = S5_WIN * QUAD_CH
S5_AHEAD = 3
TAIL_PARTS = 2
assert QUAD_CH * 2 == LANES and S5_WIN == 4
HIST = max(POOL_WINDOWS) * BATCH
CONV_HIST = (CONV_K - 1) * BATCH
assert CONV_K == 3
CONV_BLOCK = 512
V7X_VMEM_BYTES = 64 * 1024 * 1024
VMEM_LIMIT = V7X_VMEM_BYTES * 31 // 32
assert N_META * BATCH >= HIST and N_META % S5_WIN == 0

F32 = jnp.float32
BF16 = jnp.bfloat16


def _rms(h, g):
    ms = jnp.mean(h * h, axis=-1, keepdims=True)
    return h * lax.rsqrt(ms + EPS) * g


def _dot(a, b):
    return jnp.dot(a, b, preferred_element_type=F32)


def _swap_halves(v):
    return pltpu.roll(v, LANES // 2, 1)


def _s5_kernel(*refs, first, final):
    if first:
        head_ref, refs = refs[0], refs[1:]
    (h_ref, g_ref, win_ref, wb_ref, wc_ref, wt_ref, awr_ref, awi_ref, dsk_ref,
     wglu_ref, bglu_ref, wout_ref, fg_ref, o_ref, uw_ref, xs_ref, carry_ref) = refs
    i = pl.program_id(0)

    def run(tq, h, store):
        rows, win_rows, n_win = tq * BATCH, tq * BATCH // S5_WIN, tq // S5_WIN
        nb = _rms(h, g_ref[...]).astype(BF16)
        uz = _dot(nb, win_ref[...])
        u = uz[:, :D_MODEL]
        z = uz[:, D_MODEL:]

        low = lax.broadcasted_iota(jnp.int32, (win_rows, LANES), 1) < QUAD_CH
        ut = u.reshape(n_win, S5_WIN, BATCH, D_MODEL)
        for jj in range(S5_WIN // 2):
            ta = ut[:, 2 * jj].reshape(win_rows, D_MODEL)
            tb = ut[:, 2 * jj + 1].reshape(win_rows, D_MODEL)
            for col in range(D_MODEL // LANES):
                a = ta[:, col * LANES:(col + 1) * LANES]
                b = tb[:, col * LANES:(col + 1) * LANES]
                even = jnp.where(low, a, _swap_halves(b))
                odd = jnp.where(low, _swap_halves(a), b)
                uw_ref[0:win_rows, 2 * col * QUAD_IN + jj * LANES:2 * col * QUAD_IN + (jj + 1) * LANES] = (
                    even.astype(BF16))
                uw_ref[0:win_rows, (2 * col + 1) * QUAD_IN + jj * LANES:(2 * col + 1) * QUAD_IN + (jj + 1) * LANES] = (
                    odd.astype(BF16))

        def window_inputs(q):
            xs_ref[0:win_rows, 2 * q * QUAD_ST:2 * (q + 1) * QUAD_ST] = _dot(
                uw_ref[0:win_rows, q * QUAD_IN:(q + 1) * QUAD_IN], wb_ref[q])

        for q in range(S5_AHEAD):
            window_inputs(q)
        yw = []
        for q in range(QUADS):
            if q + S5_AHEAD < QUADS:
                window_inputs(q + S5_AHEAD)
            st = 2 * q * QUAD_ST
            uq = uw_ref[0:win_rows, q * QUAD_IN:(q + 1) * QUAD_IN]
            awr = jnp.broadcast_to(awr_ref[q:q + 1, :], (BATCH, QUAD_ST))
            awi = jnp.broadcast_to(awi_ref[q:q + 1, :], (BATCH, QUAD_ST))
            sr = carry_ref[0, :, q * QUAD_ST:(q + 1) * QUAD_ST]
            si = carry_ref[1, :, q * QUAD_ST:(q + 1) * QUAD_ST]
            for m in range(n_win):
                r = m * BATCH
                xr = xs_ref[r:r + BATCH, st:st + QUAD_ST]
                xi = xs_ref[r:r + BATCH, st + QUAD_ST:st + 2 * QUAD_ST]
                xs_ref[r:r + BATCH, st:st + QUAD_ST] = sr
                xs_ref[r:r + BATCH, st + QUAD_ST:st + 2 * QUAD_ST] = si
                sr, si = awr * sr - awi * si + xr, awr * si + awi * sr + xi
            carry_ref[0, :, q * QUAD_ST:(q + 1) * QUAD_ST] = sr
            carry_ref[1, :, q * QUAD_ST:(q + 1) * QUAD_ST] = si
            if store is None:
                continue
            s_in = xs_ref[0:win_rows, st:st + 2 * QUAD_ST].astype(BF16)
            yw.append(_dot(s_in, wc_ref[q]) + _dot(uq, wt_ref[q]))
        if store is None:
            return

        cols = [[None] * (D_MODEL // LANES) for _ in range(S5_WIN)]
        for col in range(D_MODEL // LANES):
            for jj in range(S5_WIN // 2):
                p = yw[2 * col][:, jj * LANES:(jj + 1) * LANES]
                q = yw[2 * col + 1][:, jj * LANES:(jj + 1) * LANES]
                cols[2 * jj][col] = jnp.where(low, p, _swap_halves(q))
                cols[2 * jj + 1][col] = jnp.where(low, _swap_halves(p), q)
        yt = [jnp.concatenate(c, axis=1).reshape(n_win, BATCH, D_MODEL) for c in cols]
        y = jnp.stack(yt, axis=1).reshape(rows, D_MODEL) + dsk_ref[...] * u
        part_t = tq // TAIL_PARTS
        parts = [slice(k * part_t * BATCH, (k + 1) * part_t * BATCH) for k in range(TAIL_PARTS)]
        gelus = [jax.nn.gelu(y[p]) for p in parts]
        gates = [_dot(yg.astype(BF16), wglu_ref[...]) for yg in gelus]
        for k, p in enumerate(parts):
            yh = gelus[k] * jax.nn.sigmoid(gates[k] + bglu_ref[...])
            yh = yh * jax.nn.silu(z[p])
            hn = h[p] + _dot(yh.astype(BF16), wout_ref[...])
            if final:
                hn = _rms(hn, fg_ref[...])
            store(k * part_t, part_t, hn)

    def store_rows(t0):
        def store(t, nt, hn):
            o_ref[(t0 + t) * BATCH:(t0 + t + nt) * BATCH, :] = hn
        return store

    def store_batch_major(t, nt, hn):
        o_ref[:, t:t + nt, :] = jnp.swapaxes(hn.reshape(nt, BATCH, D_MODEL), 0, 1)

    @pl.when(i == 0)
    def _():
        carry_ref[...] = jnp.zeros_like(carry_ref)
        h = head_ref[...] if first else h_ref[PAD * BATCH:ROWS, :]
        if final:
            run(N_META, h, None)
        else:
            o_ref[0:PAD * BATCH, :] = jnp.zeros((PAD * BATCH, D_MODEL), F32)
            run(N_META, h, store_rows(PAD))

    @pl.when(i > 0)
    def _():
        if first:
            h = jnp.swapaxes(h_ref[...], 0, 1).reshape(ROWS, D_MODEL)
        else:
            h = h_ref[...]
        run(TQ, h, store_batch_major if final else store_rows(0))


def _conv_kernel(h_ref, g_ref, win_ref, cw_ref, cb_ref, wout_ref, o_ref, hc_ref, y_ref):
    i = pl.program_id(0)
    cw = CONV_BLOCK

    def run(tq, t0):
        rows, r0 = tq * BATCH, t0 * BATCH
        h = h_ref[r0:r0 + rows, :]
        nb = _rms(h, g_ref[...]).astype(BF16)
        for c in range(CONV_E // cw):
            lo = c * cw
            bg = _dot(nb, win_ref[:, lo:lo + cw])
            cg = _dot(nb, win_ref[:, CONV_E + lo:CONV_E + lo + cw])
            v = _dot(nb, win_ref[:, 2 * CONV_E + lo:2 * CONV_E + lo + cw])
            z = _dot(nb, win_ref[:, 3 * CONV_E + lo:3 * CONV_E + lo + cw])
            hc = cg * v
            hc_ref[CONV_HIST:CONV_HIST + rows, lo:lo + cw] = hc
            conv = (cw_ref[2:3, lo:lo + cw] * hc
                    + cw_ref[1:2, lo:lo + cw] * hc_ref[BATCH:BATCH + rows, lo:lo + cw]
                    + cw_ref[0:1, lo:lo + cw] * hc_ref[0:rows, lo:lo + cw]
                    + cb_ref[:, lo:lo + cw])
            hc_ref[0:CONV_HIST, lo:lo + cw] = hc[rows - CONV_HIST:rows]
            y_ref[0:rows, lo:lo + cw] = (bg * conv * jax.nn.silu(z)).astype(BF16)
        o_ref[r0:r0 + rows, :] = h + _dot(y_ref[0:rows, :], wout_ref[...])

    @pl.when(i == 0)
    def _():
        hc_ref[0:CONV_HIST, :] = jnp.zeros((CONV_HIST, CONV_E), F32)
        o_ref[0:PAD * BATCH, :] = jnp.zeros((PAD * BATCH, D_MODEL), F32)
        run(N_META, PAD)

    @pl.when(i > 0)
    def _():
        run(TQ, 0)


def _window_sum(ref, rows, lo, width, w):
    a = ref[HIST - (w - 1) * BATCH:HIST + rows, lo:lo + width]
    span = 1
    while span < w:
        sh = span * BATCH
        a = a[sh:] + a[:-sh]
        span *= 2
    return a


def _pool_kernel(h_ref, g_ref, win_ref, wgrp_ref, bgrp_ref, scale_ref, wout_ref, o_ref, ub_ref, y_ref):
    i = pl.program_id(0)

    def run(tq, t0):
        rows, r0 = tq * BATCH, t0 * BATCH
        h = h_ref[r0:r0 + rows, :]
        nb = _rms(h, g_ref[...]).astype(BF16)
        pos = i * TQ + t0 - PAD + 1 + (lax.broadcasted_iota(jnp.int32, (rows, POOL_GROUP), 0) // BATCH)
        for k, w in enumerate(POOL_WINDOWS):
            lo = k * POOL_GROUP
            u = _dot(nb, win_ref[:, lo:lo + POOL_GROUP])
            z = _dot(nb, win_ref[:, POOL_E + lo:POOL_E + lo + POOL_GROUP])
            ub_ref[HIST:HIST + rows, lo:lo + POOL_GROUP] = u
            ws = _window_sum(ub_ref, rows, lo, POOL_GROUP, w)
            ub_ref[0:HIST, lo:lo + POOL_GROUP] = u[rows - HIST:rows]
            cnt = jnp.minimum(pos, w).astype(F32)
            mixed = ws / cnt - u
            o = _dot(mixed.astype(BF16), wgrp_ref[k]) + bgrp_ref[k:k + 1, :]
            y = o * scale_ref[:, lo:lo + POOL_GROUP]
            y_ref[0:rows, lo:lo + POOL_GROUP] = (y * jax.nn.silu(z)).astype(BF16)
        o_ref[r0:r0 + rows, :] = h + _dot(y_ref[0:rows, :], wout_ref[...])

    @pl.when(i == 0)
    def _():
        ub_ref[0:HIST, :] = jnp.zeros((HIST, POOL_E), F32)
        o_ref[0:PAD * BATCH, :] = jnp.zeros((PAD * BATCH, D_MODEL), F32)
        run(N_META, PAD)

    @pl.when(i > 0)
    def _():
        run(TQ, 0)


def _const_spec(c):
    if isinstance(c, tuple):
        arr, layer = c
        tail = arr.ndim - 1
        return pl.BlockSpec((None,) + arr.shape[1:], lambda i: (layer,) + (0,) * tail, pipeline_mode=pl.Buffered(1))
    nd = c.ndim
    return pl.BlockSpec(c.shape, lambda i: (0,) * nd, pipeline_mode=pl.Buffered(1))


_TIME_MAJOR = pl.BlockSpec((ROWS, D_MODEL), lambda i: (i, 0))
_BATCH_MAJOR = pl.BlockSpec((BATCH, TQ, D_MODEL), lambda i: (0, jnp.maximum(i - 1, 0), 0))


def _call(body, name, h, consts, scratch, *, head=None, batch_major_out=False):
    pre = [] if head is None else [head]
    out_shape = (BATCH, SEQ, D_MODEL) if batch_major_out else (L_PAD * BATCH, D_MODEL)
    return pl.pallas_call(
        body,
        out_shape=jax.ShapeDtypeStruct(out_shape, h.dtype),
        grid=(N_CHUNKS,),
        in_specs=([_const_spec(p) for p in pre] + [_BATCH_MAJOR if pre else _TIME_MAJOR]
                  + [_const_spec(c) for c in consts]),
        out_specs=_BATCH_MAJOR if batch_major_out else _TIME_MAJOR,
        scratch_shapes=scratch,
        compiler_params=pltpu.CompilerParams(
            dimension_semantics=("arbitrary",), vmem_limit_bytes=VMEM_LIMIT),
        name=name,
    )(*pre, h, *[c[0] if isinstance(c, tuple) else c for c in consts])


def _row(v):
    return v.astype(F32).reshape(1, -1)


def _s5_weights(lam_re, lam_im, log_dt, b_re, b_im, c_re, c_im):
    lr = lam_re.astype(F32)
    li = lam_im.astype(F32)
    dt = jnp.exp(log_dt.astype(F32))[:, None]
    mag = jnp.exp(lr * dt)
    ar = mag * jnp.cos(li * dt)
    ai = mag * jnp.sin(li * dt)
    den = lr * lr + li * li
    kr = ((ar - 1.0) * lr + ai * li) / den
    ki = (ai * lr - (ar - 1.0) * li) / den
    br = b_re.astype(F32)
    bi = b_im.astype(F32)
    bbr = kr[..., None] * br - ki[..., None] * bi
    bbi = kr[..., None] * bi + ki[..., None] * br
    cr = c_re.astype(F32)
    ci = c_im.astype(F32)
    pr = [jnp.ones_like(ar)]
    pi = [jnp.zeros_like(ar)]
    for _ in range(S5_WIN):
        pr, pi = pr + [pr[-1] * ar - pi[-1] * ai], pi + [pr[-1] * ai + pi[-1] * ar]
    w, nq, ng = S5_WIN, QUADS, QUAD_GROUPS

    def lanes(v):
        return v.reshape(nq, 1, QUAD_ST)

    def by_state(v):
        return v.reshape(nq, ng, -1, S5_STATE).transpose(0, 2, 1, 3).reshape(nq, -1, QUAD_ST)

    def block_diag(v, width):
        v = v.astype(BF16)
        lane_group = (lax.broadcasted_iota(jnp.int32, (ng, v.shape[-1]), 1) // width) % ng
        keep = (lane_group == lax.broadcasted_iota(jnp.int32, (ng, v.shape[-1]), 0)).astype(v.dtype)
        out = v[:, :, None] * keep[None, None, :, None, :]
        return out.reshape(nq, -1, v.shape[-1])

    br_t = by_state(jnp.swapaxes(bbr, 1, 2))
    bi_t = by_state(jnp.swapaxes(bbi, 1, 2))
    cr_t = by_state(cr)
    ci_t = by_state(ci)
    pr_l = [lanes(v) for v in pr]
    pi_l = [lanes(v) for v in pi]

    wb = jnp.stack([jnp.concatenate([pr_l[w - 1 - j] * br_t - pi_l[w - 1 - j] * bi_t,
                                     pr_l[w - 1 - j] * bi_t + pi_l[w - 1 - j] * br_t], axis=-1)
                    for j in range(w)], axis=1)
    wb = block_diag(wb, S5_STATE)

    wc = jnp.stack([jnp.concatenate([cr_t * pr_l[r + 1] - ci_t * pi_l[r + 1],
                                     -(cr_t * pi_l[r + 1] + ci_t * pr_l[r + 1])], axis=-1)
                    for r in range(w)], axis=1)
    wc = jnp.swapaxes(block_diag(wc, S5_STATE), 1, 2)

    hp = lax.Precision.HIGHEST
    taps = [jnp.einsum("gop,gpi->gio", cr * pr[d][:, None, :] - ci * pi[d][:, None, :], bbr, precision=hp)
            - jnp.einsum("gop,gpi->gio", cr * pi[d][:, None, :] + ci * pr[d][:, None, :], bbi, precision=hp)
            for d in range(w)]
    taps = [v.reshape(nq, ng, S5_GROUP, S5_GROUP).transpose(0, 2, 1, 3).reshape(nq, S5_GROUP, QUAD_CH)
            for v in taps]
    zero = jnp.zeros_like(taps[0])
    wt = jnp.stack([jnp.concatenate([taps[r - j] if r >= j else zero for r in range(w)], axis=-1)
                    for j in range(w)], axis=1)
    wt = block_diag(wt, S5_GROUP)

    return wb, wc, wt, pr[w].reshape(nq, QUAD_ST), pi[w].reshape(nq, QUAD_ST)


def _s5_layer(h, g, w_in, window_weights, d_skip, w_glu, b_glu, w_out, name, *, head=None, final_g=None):
    first = head is not None
    final = final_g is not None
    fg = _row(final_g) if final else jnp.ones((1, D_MODEL), F32)
    consts = [_row(g), w_in.astype(BF16), *window_weights, _row(d_skip), w_glu.astype(BF16), _row(b_glu),
              w_out.astype(BF16), fg]
    scratch = [pltpu.VMEM((WIN_ROWS, QUADS * QUAD_IN), BF16),
               pltpu.VMEM((WIN_ROWS, 2 * QUADS * QUAD_ST), F32),
               pltpu.VMEM((2, BATCH, QUADS * QUAD_ST), F32)]
    return _call(functools.partial(_s5_kernel, first=first, final=final), name, h, consts, scratch, head=head,
                 batch_major_out=final)


def _conv_layer(h, g, w_in, conv_w, conv_b, w_out, name):
    consts = [_row(g), w_in.astype(BF16), conv_w.astype(F32), _row(conv_b), w_out.astype(BF16)]
    scratch = [pltpu.VMEM((CONV_HIST + ROWS, CONV_E), F32), pltpu.VMEM((ROWS, CONV_E), BF16)]
    return _call(_conv_kernel, name, h, consts, scratch)


def _pool_layer(h, g, w_in, w_grp, b_grp, scale, w_out, name):
    consts = [_row(g), w_in.astype(BF16), w_grp.astype(BF16), b_grp.astype(F32), _row(scale),
              w_out.astype(BF16)]
    scratch = [pltpu.VMEM((HIST + ROWS, POOL_E), F32), pltpu.VMEM((ROWS, POOL_E), BF16)]
    return _call(_pool_kernel, name, h, consts, scratch)


def kernel(x, meta_tokens, norm0_g, l0_w_in, l0_lam_re, l0_lam_im, l0_log_dt, l0_b_re, l0_b_im, l0_c_re, l0_c_im, l0_d_skip, l0_w_glu, l0_b_glu, l0_w_out, norm1_g, l1_w_in, l1_conv_w, l1_conv_b, l1_w_out, norm2_g, l2_w_in, l2_w_grp, l2_b_grp, l2_scale, l2_w_out, norm3_g, l3_w_in, l3_lam_re, l3_lam_im, l3_log_dt, l3_b_re, l3_b_im, l3_c_re, l3_c_im, l3_d_skip, l3_w_glu, l3_b_glu, l3_w_out, final_g):
    assert x.shape == (BATCH, SEQ, D_MODEL) and meta_tokens.shape == (N_META, D_MODEL)
    head = jnp.broadcast_to(meta_tokens.astype(x.dtype)[:, None, :], (N_META, BATCH, D_MODEL))
    head = head.reshape(N_META * BATCH, D_MODEL)
    s5_params = [(l0_lam_re, l3_lam_re), (l0_lam_im, l3_lam_im), (l0_log_dt, l3_log_dt), (l0_b_re, l3_b_re),
                 (l0_b_im, l3_b_im), (l0_c_re, l3_c_re), (l0_c_im, l3_c_im)]
    stacked = jax.vmap(_s5_weights)(*[jnp.stack(p) for p in s5_params])
    h = _s5_layer(x, norm0_g, l0_w_in, [(w, 0) for w in stacked], l0_d_skip, l0_w_glu, l0_b_glu, l0_w_out,
                  "s5_layer0", head=head)
    h = _conv_layer(h, norm1_g, l1_w_in, l1_conv_w, l1_conv_b, l1_w_out, "conv_layer1")
    h = _pool_layer(h, norm2_g, l2_w_in, l2_w_grp, l2_b_grp, l2_scale, l2_w_out, "pool_layer2")
    return _s5_layer(h, norm3_g, l3_w_in, [(w, 1) for w in stacked], l3_d_skip, l3_w_glu, l3_b_glu, l3_w_out,
                     "s5_layer3", final_g=final_g)
```

```python
import functools

import jax
import jax.numpy as jnp
from jax import lax
from jax.experimental import pallas as pl
from jax.experimental.pallas import tpu as pltpu

D_MODEL = 1024
BATCH = 8
SEQ = 4096
N_META = 16
EPS = 1e-6
S5_GROUP = 16
S5_GROUPS = 64
S5_STATE = 64
CONV_E = 2048
CONV_K = 3
POOL_E = 2048
POOL_WINDOWS = (2, 4, 8, 16)
POOL_GROUP = 512

TQ = 128
ROWS = TQ * BATCH
PAD = TQ - N_META
L_PAD = SEQ + TQ
N_CHUNKS = L_PAD // TQ
LANES = 128
S5_WIN = 4
WIN_ROWS = ROWS // S5_WIN
QUAD_GROUPS = 4
QUADS = S5_GROUPS // QUAD_GROUPS
QUAD_CH = QUAD_GROUPS * S5_GROUP
QUAD_ST = QUAD_GROUPS * S5_STATE
QUAD_IN = S5_WIN * QUAD_CH
S5_AHEAD = 3
TAIL_PARTS = 2
assert QUAD_CH * 2 == LANES and S5_WIN == 4
HIST = max(POOL_WINDOWS) * BATCH
CONV_HIST = (CONV_K - 1) * BATCH
assert CONV_K == 3
CONV_BLOCK = 512
V7X_VMEM_BYTES = 64 * 1024 * 1024
VMEM_LIMIT = V7X_VMEM_BYTES * 31 // 32
assert N_META % (S5_WIN * TAIL_PARTS) == 0

F32 = jnp.float32
BF16 = jnp.bfloat16


def _rms(h, g):
    ms = jnp.mean(h * h, axis=-1, keepdims=True)
    return h * lax.rsqrt(ms + EPS) * g


def _dot(a, b):
    return jnp.dot(a, b, preferred_element_type=F32)


def _swap_halves(v):
    return pltpu.roll(v, LANES // 2, 1)


def _s5_kernel(*refs, first, final):
    if first:
        head_ref, refs = refs[0], refs[1:]
    (h_ref, g_ref, win_ref, wb_ref, wc_ref, wt_ref, awr_ref, awi_ref, dsk_ref,
     wglu_ref, bglu_ref, wout_ref, fg_ref, o_ref, uw_ref, xs_ref, carry_ref) = refs
    i = pl.program_id(0)

    def run(tq, h, store):
        rows, win_rows, n_win = tq * BATCH, tq * BATCH // S5_WIN, tq // S5_WIN
        nb = _rms(h, g_ref[...]).astype(BF16)
        uz = _dot(nb, win_ref[...])
        u = uz[:, :D_MODEL]
        z = uz[:, D_MODEL:]

        low = lax.broadcasted_iota(jnp.int32, (win_rows, LANES), 1) < QUAD_CH
        ut = u.reshape(n_win, S5_WIN, BATCH, D_MODEL)
        for jj in range(S5_WIN // 2):
            ta = ut[:, 2 * jj].reshape(win_rows, D_MODEL)
            tb = ut[:, 2 * jj + 1].reshape(win_rows, D_MODEL)
            for col in range(D_MODEL // LANES):
                a = ta[:, col * LANES:(col + 1) * LANES]
                b = tb[:, col * LANES:(col + 1) * LANES]
                even = jnp.where(low, a, _swap_halves(b))
                odd = jnp.where(low, _swap_halves(a), b)
                uw_ref[0:win_rows, 2 * col * QUAD_IN + jj * LANES:2 * col * QUAD_IN + (jj + 1) * LANES] = (
                    even.astype(BF16))
                uw_ref[0:win_rows, (2 * col + 1) * QUAD_IN + jj * LANES:(2 * col + 1) * QUAD_IN + (jj + 1) * LANES] = (
                    odd.astype(BF16))

        def window_inputs(q):
            xs_ref[0:win_rows, 2 * q * QUAD_ST:2 * (q + 1) * QUAD_ST] = _dot(
                uw_ref[0:win_rows, q * QUAD_IN:(q + 1) * QUAD_IN], wb_ref[q])

        for q in range(S5_AHEAD):
            window_inputs(q)
        yw = []
        for q in range(QUADS):
            if q + S5_AHEAD < QUADS:
                window_inputs(q + S5_AHEAD)
            st = 2 * q * QUAD_ST
            uq = uw_ref[0:win_rows, q * QUAD_IN:(q + 1) * QUAD_IN]
            awr = jnp.broadcast_to(awr_ref[q:q + 1, :], (BATCH, QUAD_ST))
            awi = jnp.broadcast_to(awi_ref[q:q + 1, :], (BATCH, QUAD_ST))
            sr = carry_ref[0, :, q * QUAD_ST:(q + 1) * QUAD_ST]
            si = carry_ref[1, :, q * QUAD_ST:(q + 1) * QUAD_ST]
            for m in range(n_win):
                r = m * BATCH
                xr = xs_ref[r:r + BATCH, st:st + QUAD_ST]
                xi = xs_ref[r:r + BATCH, st + QUAD_ST:st + 2 * QUAD_ST]
                xs_ref[r:r + BATCH, st:st + QUAD_ST] = sr
                xs_ref[r:r + BATCH, st + QUAD_ST:st + 2 * QUAD_ST] = si
                sr, si = awr * sr - awi * si + xr, awr * si + awi * sr + xi
            carry_ref[0, :, q * QUAD_ST:(q + 1) * QUAD_ST] = sr
            carry_ref[1, :, q * QUAD_ST:(q + 1) * QUAD_ST] = si
            if store is None:
                continue
            s_in = xs_ref[0:win_rows, st:st + 2 * QUAD_ST].astype(BF16)
            yw.append(_dot(s_in, wc_ref[q]) + _dot(uq, wt_ref[q]))
        if store is None:
            return

        cols = [[None] * (D_MODEL // LANES) for _ in range(S5_WIN)]
        for col in range(D_MODEL // LANES):
            for jj in range(S5_WIN // 2):
                p = yw[2 * col][:, jj * LANES:(jj + 1) * LANES]
                q = yw[2 * col + 1][:, jj * LANES:(jj + 1) * LANES]
                cols[2 * jj][col] = jnp.where(low, p, _swap_halves(q))
                cols[2 * jj + 1][col] = jnp.where(low, _swap_halves(p), q)
        yt = [jnp.concatenate(c, axis=1).reshape(n_win, BATCH, D_MODEL) for c in cols]
        y = jnp.stack(yt, axis=1).reshape(rows, D_MODEL) + dsk_ref[...] * u
        part_t = tq // TAIL_PARTS
        parts = [slice(k * part_t * BATCH, (k + 1) * part_t * BATCH) for k in range(TAIL_PARTS)]
        gelus = [jax.nn.gelu(y[p]) for p in parts]
        gates = [_dot(yg.astype(BF16), wglu_ref[...]) for yg in gelus]
        for k, p in enumerate(parts):
            yh = gelus[k] * jax.nn.sigmoid(gates[k] + bglu_ref[...])
            yh = yh * jax.nn.silu(z[p])
            hn = h[p] + _dot(yh.astype(BF16), wout_ref[...])
            if final:
                hn = _rms(hn, fg_ref[...])
            store(k * part_t, part_t, hn)

    def store_rows(t0):
        def store(t, nt, hn):
            o_ref[(t0 + t) * BATCH:(t0 + t + nt) * BATCH, :] = hn
        return store

    def store_batch_major(t, nt, hn):
        o_ref[:, t:t + nt, :] = jnp.swapaxes(hn.reshape(nt, BATCH, D_MODEL), 0, 1)

    @pl.when(i == 0)
    def _():
        carry_ref[...] = jnp.zeros_like(carry_ref)
        h = head_ref[...] if first else h_ref[PAD * BATCH:ROWS, :]
        if final:
            run(N_META, h, None)
        else:
            o_ref[0:PAD * BATCH, :] = jnp.zeros((PAD * BATCH, D_MODEL), F32)
            run(N_META, h, store_rows(PAD))

    @pl.when(i > 0)
    def _():
        if first:
            h = jnp.swapaxes(h_ref[...], 0, 1).reshape(ROWS, D_MODEL)
        else:
            h = h_ref[...]
        run(TQ, h, store_batch_major if final else store_rows(0))


def _conv_kernel(h_ref, g_ref, win_ref, cw_ref, cb_ref, wout_ref, o_ref, hc_ref, y_ref):
    @pl.when(pl.program_id(0) == 0)
    def _():
        hc_ref[0:CONV_HIST, :] = jnp.zeros((CONV_HIST, CONV_E), F32)

    h = h_ref[...]
    nb = _rms(h, g_ref[...]).astype(BF16)
    cw = CONV_BLOCK
    for c in range(CONV_E // cw):
        lo = c * cw
        bg = _dot(nb, win_ref[:, lo:lo + cw])
        cg = _dot(nb, win_ref[:, CONV_E + lo:CONV_E + lo + cw])
        v = _dot(nb, win_ref[:, 2 * CONV_E + lo:2 * CONV_E + lo + cw])
        z = _dot(nb, win_ref[:, 3 * CONV_E + lo:3 * CONV_E + lo + cw])
        hc = cg * v
        hc_ref[CONV_HIST:CONV_HIST + ROWS, lo:lo + cw] = hc
        conv = (cw_ref[2:3, lo:lo + cw] * hc
                + cw_ref[1:2, lo:lo + cw] * hc_ref[BATCH:BATCH + ROWS, lo:lo + cw]
                + cw_ref[0:1, lo:lo + cw] * hc_ref[0:ROWS, lo:lo + cw]
                + cb_ref[:, lo:lo + cw])
        hc_ref[0:CONV_HIST, lo:lo + cw] = hc[ROWS - CONV_HIST:ROWS]
        y_ref[:, lo:lo + cw] = (bg * conv * jax.nn.silu(z)).astype(BF16)
    o_ref[...] = h + _dot(y_ref[...], wout_ref[...])


def _window_sum(ref, lo, width, w):
    a = ref[HIST - (w - 1) * BATCH:HIST + ROWS, lo:lo + width]
    span = 1
    while span < w:
        sh = span * BATCH
        a = a[sh:] + a[:-sh]
        span *= 2
    return a


def _pool_kernel(h_ref, g_ref, win_ref, wgrp_ref, bgrp_ref, scale_ref, wout_ref, o_ref, ub_ref, y_ref):
    i = pl.program_id(0)

    @pl.when(i == 0)
    def _():
        ub_ref[0:HIST, :] = jnp.zeros((HIST, POOL_E), F32)

    h = h_ref[...]
    nb = _rms(h, g_ref[...]).astype(BF16)
    pos = i * TQ - PAD + 1 + (lax.broadcasted_iota(jnp.int32, (ROWS, POOL_GROUP), 0) // BATCH)
    for k, w in enumerate(POOL_WINDOWS):
        lo = k * POOL_GROUP
        u = _dot(nb, win_ref[:, lo:lo + POOL_GROUP])
        z = _dot(nb, win_ref[:, POOL_E + lo:POOL_E + lo + POOL_GROUP])
        ub_ref[HIST:HIST + ROWS, lo:lo + POOL_GROUP] = u
        ws = _window_sum(ub_ref, lo, POOL_GROUP, w)
        ub_ref[0:HIST, lo:lo + POOL_GROUP] = u[ROWS - HIST:ROWS]
        cnt = jnp.clip(pos, 1, w).astype(F32)
        mixed = ws / cnt - u
        o = _dot(mixed.astype(BF16), wgrp_ref[k]) + bgrp_ref[k:k + 1, :]
        y = o * scale_ref[:, lo:lo + POOL_GROUP]
        y_ref[:, lo:lo + POOL_GROUP] = (y * jax.nn.silu(z)).astype(BF16)
    o_ref[...] = h + _dot(y_ref[...], wout_ref[...])


def _const_spec(c):
    if isinstance(c, tuple):
        arr, layer = c
        tail = arr.ndim - 1
        return pl.BlockSpec((None,) + arr.shape[1:], lambda i: (layer,) + (0,) * tail, pipeline_mode=pl.Buffered(1))
    nd = c.ndim
    return pl.BlockSpec(c.shape, lambda i: (0,) * nd, pipeline_mode=pl.Buffered(1))


_TIME_MAJOR = pl.BlockSpec((ROWS, D_MODEL), lambda i: (i, 0))
_BATCH_MAJOR = pl.BlockSpec((BATCH, TQ, D_MODEL), lambda i: (0, jnp.maximum(i - 1, 0), 0))


def _call(body, name, h, consts, scratch, *, head=None, batch_major_out=False):
    pre = [] if head is None else [head]
    out_shape = (BATCH, SEQ, D_MODEL) if batch_major_out else (L_PAD * BATCH, D_MODEL)
    return pl.pallas_call(
        body,
        out_shape=jax.ShapeDtypeStruct(out_shape, h.dtype),
        grid=(N_CHUNKS,),
        in_specs=([_const_spec(p) for p in pre] + [_BATCH_MAJOR if pre else _TIME_MAJOR]
                  + [_const_spec(c) for c in consts]),
        out_specs=_BATCH_MAJOR if batch_major_out else _TIME_MAJOR,
        scratch_shapes=scratch,
        compiler_params=pltpu.CompilerParams(
            dimension_semantics=("arbitrary",), vmem_limit_bytes=VMEM_LIMIT),
        name=name,
    )(*pre, h, *[c[0] if isinstance(c, tuple) else c for c in consts])


def _row(v):
    return v.astype(F32).reshape(1, -1)


def _s5_weights(lam_re, lam_im, log_dt, b_re, b_im, c_re, c_im):
    lr = lam_re.astype(F32)
    li = lam_im.astype(F32)
    dt = jnp.exp(log_dt.astype(F32))[:, None]
    mag = jnp.exp(lr * dt)
    ar = mag * jnp.cos(li * dt)
    ai = mag * jnp.sin(li * dt)
    den = lr * lr + li * li
    kr = ((ar - 1.0) * lr + ai * li) / den
    ki = (ai * lr - (ar - 1.0) * li) / den
    br = b_re.astype(F32)
    bi = b_im.astype(F32)
    bbr = kr[..., None] * br - ki[..., None] * bi
    bbi = kr[..., None] * bi + ki[..., None] * br
    cr = c_re.astype(F32)
    ci = c_im.astype(F32)
    pr = [jnp.ones_like(ar)]
    pi = [jnp.zeros_like(ar)]
    for _ in range(S5_WIN):
        pr, pi = pr + [pr[-1] * ar - pi[-1] * ai], pi + [pr[-1] * ai + pi[-1] * ar]
    w, nq, ng = S5_WIN, QUADS, QUAD_GROUPS

    def lanes(v):
        return v.reshape(nq, 1, QUAD_ST)

    def by_state(v):
        return v.reshape(nq, ng, -1, S5_STATE).transpose(0, 2, 1, 3).reshape(nq, -1, QUAD_ST)

    def block_diag(v, width):
        v = v.astype(BF16)
        lane_group = (lax.broadcasted_iota(jnp.int32, (ng, v.shape[-1]), 1) // width) % ng
        keep = (lane_group == lax.broadcasted_iota(jnp.int32, (ng, v.shape[-1]), 0)).astype(v.dtype)
        out = v[:, :, None] * keep[None, None, :, None, :]
        return out.reshape(nq, -1, v.shape[-1])

    br_t = by_state(jnp.swapaxes(bbr, 1, 2))
    bi_t = by_state(jnp.swapaxes(bbi, 1, 2))
    cr_t = by_state(cr)
    ci_t = by_state(ci)
    pr_l = [lanes(v) for v in pr]
    pi_l = [lanes(v) for v in pi]

    wb = jnp.stack([jnp.concatenate([pr_l[w - 1 - j] * br_t - pi_l[w - 1 - j] * bi_t,
                                     pr_l[w - 1 - j] * bi_t + pi_l[w - 1 - j] * br_t], axis=-1)
                    for j in range(w)], axis=1)
    wb = block_diag(wb, S5_STATE)

    wc = jnp.stack([jnp.concatenate([cr_t * pr_l[r + 1] - ci_t * pi_l[r + 1],
                                     -(cr_t * pi_l[r + 1] + ci_t * pr_l[r + 1])], axis=-1)
                    for r in range(w)], axis=1)
    wc = jnp.swapaxes(block_diag(wc, S5_STATE), 1, 2)

    hp = lax.Precision.HIGHEST
    taps = [jnp.einsum("gop,gpi->gio", cr * pr[d][:, None, :] - ci * pi[d][:, None, :], bbr, precision=hp)
            - jnp.einsum("gop,gpi->gio", cr * pi[d][:, None, :] + ci * pr[d][:, None, :], bbi, precision=hp)
            for d in range(w)]
    taps = [v.reshape(nq, ng, S5_GROUP, S5_GROUP).transpose(0, 2, 1, 3).reshape(nq, S5_GROUP, QUAD_CH)
            for v in taps]
    zero = jnp.zeros_like(taps[0])
    wt = jnp.stack([jnp.concatenate([taps[r - j] if r >= j else zero for r in range(w)], axis=-1)
                    for j in range(w)], axis=1)
    wt = block_diag(wt, S5_GROUP)

    return wb, wc, wt, pr[w].reshape(nq, QUAD_ST), pi[w].reshape(nq, QUAD_ST)


def _s5_layer(h, g, w_in, window_weights, d_skip, w_glu, b_glu, w_out, name, *, head=None, final_g=None):
    first = head is not None
    final = final_g is not None
    fg = _row(final_g) if final else jnp.ones((1, D_MODEL), F32)
    consts = [_row(g), w_in.astype(BF16), *window_weights, _row(d_skip), w_glu.astype(BF16), _row(b_glu),
              w_out.astype(BF16), fg]
    scratch = [pltpu.VMEM((WIN_ROWS, QUADS * QUAD_IN), BF16),
               pltpu.VMEM((WIN_ROWS, 2 * QUADS * QUAD_ST), F32),
               pltpu.VMEM((2, BATCH, QUADS * QUAD_ST), F32)]
    return _call(functools.partial(_s5_kernel, first=first, final=final), name, h, consts, scratch, head=head,
                 batch_major_out=final)


def _conv_layer(h, g, w_in, conv_w, conv_b, w_out, name):
    consts = [_row(g), w_in.astype(BF16), conv_w.astype(F32), _row(conv_b), w_out.astype(BF16)]
    scratch = [pltpu.VMEM((CONV_HIST + ROWS, CONV_E), F32), pltpu.VMEM((ROWS, CONV_E), BF16)]
    return _call(_conv_kernel, name, h, consts, scratch)


def _pool_layer(h, g, w_in, w_grp, b_grp, scale, w_out, name):
    consts = [_row(g), w_in.astype(BF16), w_grp.astype(BF16), b_grp.astype(F32), _row(scale),
              w_out.astype(BF16)]
    scratch = [pltpu.VMEM((HIST + ROWS, POOL_E), F32), pltpu.VMEM((ROWS, POOL_E), BF16)]
    return _call(_pool_kernel, name, h, consts, scratch)


def kernel(x, meta_tokens, norm0_g, l0_w_in, l0_lam_re, l0_lam_im, l0_log_dt, l0_b_re, l0_b_im, l0_c_re, l0_c_im, l0_d_skip, l0_w_glu, l0_b_glu, l0_w_out, norm1_g, l1_w_in, l1_conv_w, l1_conv_b, l1_w_out, norm2_g, l2_w_in, l2_w_grp, l2_b_grp, l2_scale, l2_w_out, norm3_g, l3_w_in, l3_lam_re, l3_lam_im, l3_log_dt, l3_b_re, l3_b_im, l3_c_re, l3_c_im, l3_d_skip, l3_w_glu, l3_b_glu, l3_w_out, final_g):
    assert x.shape == (BATCH, SEQ, D_MODEL) and meta_tokens.shape == (N_META, D_MODEL)
    head = jnp.broadcast_to(meta_tokens.astype(x.dtype)[:, None, :], (N_META, BATCH, D_MODEL))
    head = head.reshape(N_META * BATCH, D_MODEL)
    s5_params = [(l0_lam_re, l3_lam_re), (l0_lam_im, l3_lam_im), (l0_log_dt, l3_log_dt), (l0_b_re, l3_b_re),
                 (l0_b_im, l3_b_im), (l0_c_re, l3_c_re), (l0_c_im, l3_c_im)]
    stacked = jax.vmap(_s5_weights)(*[jnp.stack(p) for p in s5_params])
    h = _s5_layer(x, norm0_g, l0_w_in, [(w, 0) for w in stacked], l0_d_skip, l0_w_glu, l0_b_glu, l0_w_out,
                  "s5_layer0", head=head)
    h = _conv_layer(h, norm1_g, l1_w_in, l1_conv_w, l1_conv_b, l1_w_out, "conv_layer1")
    h = _pool_layer(h, norm2_g, l2_w_in, l2_w_grp, l2_b_grp, l2_scale, l2_w_out, "pool_layer2")
    return _s5_layer(h, norm3_g, l3_w_in, [(w, 1) for w in stacked], l3_d_skip, l3_w_glu, l3_b_glu, l3_w_out,
                     "s5_layer3", final_g=final_g)
```

```python
import functools

import jax
import jax.numpy as jnp
from jax import lax
from jax.experimental import pallas as pl
from jax.experimental.pallas import tpu as pltpu

D_MODEL = 1024
BATCH = 8
SEQ = 4096
N_META = 16
EPS = 1e-6
S5_GROUP = 16
S5_GROUPS = 64
S5_STATE = 64
CONV_E = 2048
CONV_K = 3
POOL_E = 2048
POOL_WINDOWS = (2, 4, 8, 16)
POOL_GROUP = 512

TQ = 128
ROWS = TQ * BATCH
PAD = TQ - N_META
L_PAD = SEQ + TQ
N_CHUNKS = L_PAD // TQ
LANES = 128
S5_WIN = 4
WIN_ROWS = ROWS // S5_WIN
QUAD_GROUPS = 4
QUADS = S5_GROUPS // QUAD_GROUPS
QUAD_CH = QUAD_GROUPS * S5_GROUP
QUAD_ST = QUAD_GROUPS * S5_STATE
QUAD_IN = S5_WIN * QUAD_CH
S5_AHEAD = 3
TAIL_PARTS = 2
assert QUAD_CH * 2 == LANES and S5_WIN == 4
HIST = max(POOL_WINDOWS) * BATCH
CONV_HIST = (CONV_K - 1) * BATCH
assert CONV_K == 3
CONV_BLOCK = 256
V7X_VMEM_BYTES = 64 * 1024 * 1024
VMEM_LIMIT = V7X_VMEM_BYTES * 31 // 32
assert N_META % (S5_WIN * TAIL_PARTS) == 0

F32 = jnp.float32
BF16 = jnp.bfloat16


def _rms(h, g):
    ms = jnp.mean(h * h, axis=-1, keepdims=True)
    return h * lax.rsqrt(ms + EPS) * g


def _dot(a, b):
    return jnp.dot(a, b, preferred_element_type=F32)


def _swap_halves(v):
    return pltpu.roll(v, LANES // 2, 1)


def _s5_kernel(*refs, first, final):
    if first:
        head_ref, refs = refs[0], refs[1:]
    (h_ref, g_ref, win_ref, wb_ref, wc_ref, wt_ref, awr_ref, awi_ref, dsk_ref,
     wglu_ref, bglu_ref, wout_ref, fg_ref, o_ref, uw_ref, xs_ref, carry_ref) = refs
    i = pl.program_id(0)

    def run(tq, h, store):
        rows, win_rows, n_win = tq * BATCH, tq * BATCH // S5_WIN, tq // S5_WIN
        nb = _rms(h, g_ref[...]).astype(BF16)
        uz = _dot(nb, win_ref[...])
        u = uz[:, :D_MODEL]
        z = uz[:, D_MODEL:]

        low = lax.broadcasted_iota(jnp.int32, (win_rows, LANES), 1) < QUAD_CH
        ut = u.reshape(n_win, S5_WIN, BATCH, D_MODEL)
        for jj in range(S5_WIN // 2):
            ta = ut[:, 2 * jj].reshape(win_rows, D_MODEL)
            tb = ut[:, 2 * jj + 1].reshape(win_rows, D_MODEL)
            for col in range(D_MODEL // LANES):
                a = ta[:, col * LANES:(col + 1) * LANES]
                b = tb[:, col * LANES:(col + 1) * LANES]
                even = jnp.where(low, a, _swap_halves(b))
                odd = jnp.where(low, _swap_halves(a), b)
                uw_ref[0:win_rows, 2 * col * QUAD_IN + jj * LANES:2 * col * QUAD_IN + (jj + 1) * LANES] = (
                    even.astype(BF16))
                uw_ref[0:win_rows, (2 * col + 1) * QUAD_IN + jj * LANES:(2 * col + 1) * QUAD_IN + (jj + 1) * LANES] = (
                    odd.astype(BF16))

        def window_inputs(q):
            xs_ref[0:win_rows, 2 * q * QUAD_ST:2 * (q + 1) * QUAD_ST] = _dot(
                uw_ref[0:win_rows, q * QUAD_IN:(q + 1) * QUAD_IN], wb_ref[q])

        for q in range(S5_AHEAD):
            window_inputs(q)
        yw = []
        for q in range(QUADS):
            if q + S5_AHEAD < QUADS:
                window_inputs(q + S5_AHEAD)
            st = 2 * q * QUAD_ST
            uq = uw_ref[0:win_rows, q * QUAD_IN:(q + 1) * QUAD_IN]
            awr = jnp.broadcast_to(awr_ref[q:q + 1, :], (BATCH, QUAD_ST))
            awi = jnp.broadcast_to(awi_ref[q:q + 1, :], (BATCH, QUAD_ST))
            sr = carry_ref[0, :, q * QUAD_ST:(q + 1) * QUAD_ST]
            si = carry_ref[1, :, q * QUAD_ST:(q + 1) * QUAD_ST]
            for m in range(n_win):
                r = m * BATCH
                xr = xs_ref[r:r + BATCH, st:st + QUAD_ST]
                xi = xs_ref[r:r + BATCH, st + QUAD_ST:st + 2 * QUAD_ST]
                xs_ref[r:r + BATCH, st:st + QUAD_ST] = sr
                xs_ref[r:r + BATCH, st + QUAD_ST:st + 2 * QUAD_ST] = si
                sr, si = awr * sr - awi * si + xr, awr * si + awi * sr + xi
            carry_ref[0, :, q * QUAD_ST:(q + 1) * QUAD_ST] = sr
            carry_ref[1, :, q * QUAD_ST:(q + 1) * QUAD_ST] = si
            if store is None:
                continue
            s_in = xs_ref[0:win_rows, st:st + 2 * QUAD_ST].astype(BF16)
            yw.append(_dot(s_in, wc_ref[q]) + _dot(uq, wt_ref[q]))
        if store is None:
            return

        cols = [[None] * (D_MODEL // LANES) for _ in range(S5_WIN)]
        for col in range(D_MODEL // LANES):
            for jj in range(S5_WIN // 2):
                p = yw[2 * col][:, jj * LANES:(jj + 1) * LANES]
                q = yw[2 * col + 1][:, jj * LANES:(jj + 1) * LANES]
                cols[2 * jj][col] = jnp.where(low, p, _swap_halves(q))
                cols[2 * jj + 1][col] = jnp.where(low, _swap_halves(p), q)
        yt = [jnp.concatenate(c, axis=1).reshape(n_win, BATCH, D_MODEL) for c in cols]
        y = jnp.stack(yt, axis=1).reshape(rows, D_MODEL) + dsk_ref[...] * u
        part_t = tq // TAIL_PARTS
        parts = [slice(k * part_t * BATCH, (k + 1) * part_t * BATCH) for k in range(TAIL_PARTS)]
        gelus = [jax.nn.gelu(y[p]) for p in parts]
        gates = [_dot(yg.astype(BF16), wglu_ref[...]) for yg in gelus]
        for k, p in enumerate(parts):
            yh = gelus[k] * jax.nn.sigmoid(gates[k] + bglu_ref[...])
            yh = yh * jax.nn.silu(z[p])
            hn = h[p] + _dot(yh.astype(BF16), wout_ref[...])
            if final:
                hn = _rms(hn, fg_ref[...])
            store(k * part_t, part_t, hn)

    def store_rows(t0):
        def store(t, nt, hn):
            o_ref[(t0 + t) * BATCH:(t0 + t + nt) * BATCH, :] = hn
        return store

    def store_batch_major(t, nt, hn):
        o_ref[:, t:t + nt, :] = jnp.swapaxes(hn.reshape(nt, BATCH, D_MODEL), 0, 1)

    @pl.when(i == 0)
    def _():
        carry_ref[...] = jnp.zeros_like(carry_ref)
        h = head_ref[...] if first else h_ref[PAD * BATCH:ROWS, :]
        if final:
            run(N_META, h, None)
        else:
            o_ref[0:PAD * BATCH, :] = jnp.zeros((PAD * BATCH, D_MODEL), F32)
            run(N_META, h, store_rows(PAD))

    @pl.when(i > 0)
    def _():
        if first:
            h = jnp.swapaxes(h_ref[...], 0, 1).reshape(ROWS, D_MODEL)
        else:
            h = h_ref[...]
        run(TQ, h, store_batch_major if final else store_rows(0))


def _conv_kernel(h_ref, g_ref, win_ref, cw_ref, cb_ref, wout_ref, o_ref, hc_ref, y_ref):
    @pl.when(pl.program_id(0) == 0)
    def _():
        hc_ref[0:CONV_HIST, :] = jnp.zeros((CONV_HIST, CONV_E), F32)

    h = h_ref[...]
    nb = _rms(h, g_ref[...]).astype(BF16)
    cw = CONV_BLOCK
    for c in range(CONV_E // cw):
        lo = c * cw
        bg = _dot(nb, win_ref[:, lo:lo + cw])
        cg = _dot(nb, win_ref[:, CONV_E + lo:CONV_E + lo + cw])
        v = _dot(nb, win_ref[:, 2 * CONV_E + lo:2 * CONV_E + lo + cw])
        z = _dot(nb, win_ref[:, 3 * CONV_E + lo:3 * CONV_E + lo + cw])
        hc = cg * v
        hc_ref[CONV_HIST:CONV_HIST + ROWS, lo:lo + cw] = hc
        conv = (cw_ref[2:3, lo:lo + cw] * hc
                + cw_ref[1:2, lo:lo + cw] * hc_ref[BATCH:BATCH + ROWS, lo:lo + cw]
                + cw_ref[0:1, lo:lo + cw] * hc_ref[0:ROWS, lo:lo + cw]
                + cb_ref[:, lo:lo + cw])
        hc_ref[0:CONV_HIST, lo:lo + cw] = hc[ROWS - CONV_HIST:ROWS]
        y_ref[:, lo:lo + cw] = (bg * conv * jax.nn.silu(z)).astype(BF16)
    o_ref[...] = h + _dot(y_ref[...], wout_ref[...])


def _window_sum(ref, lo, width, w):
    a = ref[HIST - (w - 1) * BATCH:HIST + ROWS, lo:lo + width]
    span = 1
    while span < w:
        sh = span * BATCH
        a = a[sh:] + a[:-sh]
        span *= 2
    return a


def _pool_kernel(h_ref, g_ref, win_ref, wgrp_ref, bgrp_ref, scale_ref, wout_ref, o_ref, ub_ref, y_ref):
    i = pl.program_id(0)

    @pl.when(i == 0)
    def _():
        ub_ref[0:HIST, :] = jnp.zeros((HIST, POOL_E), F32)

    h = h_ref[...]
    nb = _rms(h, g_ref[...]).astype(BF16)
    pos = i * TQ - PAD + 1 + (lax.broadcasted_iota(jnp.int32, (ROWS, POOL_GROUP), 0) // BATCH)
    for k, w in enumerate(POOL_WINDOWS):
        lo = k * POOL_GROUP
        u = _dot(nb, win_ref[:, lo:lo + POOL_GROUP])
        z = _dot(nb, win_ref[:, POOL_E + lo:POOL_E + lo + POOL_GROUP])
        ub_ref[HIST:HIST + ROWS, lo:lo + POOL_GROUP] = u
        ws = _window_sum(ub_ref, lo, POOL_GROUP, w)
        ub_ref[0:HIST, lo:lo + POOL_GROUP] = u[ROWS - HIST:ROWS]
        cnt = jnp.clip(pos, 1, w).astype(F32)
        mixed = ws / cnt - u
        o = _dot(mixed.astype(BF16), wgrp_ref[k]) + bgrp_ref[k:k + 1, :]
        y = o * scale_ref[:, lo:lo + POOL_GROUP]
        y_ref[:, lo:lo + POOL_GROUP] = (y * jax.nn.silu(z)).astype(BF16)
    o_ref[...] = h + _dot(y_ref[...], wout_ref[...])


def _const_spec(c):
    if isinstance(c, tuple):
        arr, layer = c
        tail = arr.ndim - 1
        return pl.BlockSpec((None,) + arr.shape[1:], lambda i: (layer,) + (0,) * tail, pipeline_mode=pl.Buffered(1))
    nd = c.ndim
    return pl.BlockSpec(c.shape, lambda i: (0,) * nd, pipeline_mode=pl.Buffered(1))


_TIME_MAJOR = pl.BlockSpec((ROWS, D_MODEL), lambda i: (i, 0))
_BATCH_MAJOR = pl.BlockSpec((BATCH, TQ, D_MODEL), lambda i: (0, jnp.maximum(i - 1, 0), 0))


def _call(body, name, h, consts, scratch, *, head=None, batch_major_out=False):
    pre = [] if head is None else [head]
    out_shape = (BATCH, SEQ, D_MODEL) if batch_major_out else (L_PAD * BATCH, D_MODEL)
    return pl.pallas_call(
        body,
        out_shape=jax.ShapeDtypeStruct(out_shape, h.dtype),
        grid=(N_CHUNKS,),
        in_specs=([_const_spec(p) for p in pre] + [_BATCH_MAJOR if pre else _TIME_MAJOR]
                  + [_const_spec(c) for c in consts]),
        out_specs=_BATCH_MAJOR if batch_major_out else _TIME_MAJOR,
        scratch_shapes=scratch,
        compiler_params=pltpu.CompilerParams(
            dimension_semantics=("arbitrary",), vmem_limit_bytes=VMEM_LIMIT),
        name=name,
    )(*pre, h, *[c[0] if isinstance(c, tuple) else c for c in consts])


def _row(v):
    return v.astype(F32).reshape(1, -1)


def _s5_weights(lam_re, lam_im, log_dt, b_re, b_im, c_re, c_im):
    lr = lam_re.astype(F32)
    li = lam_im.astype(F32)
    dt = jnp.exp(log_dt.astype(F32))[:, None]
    mag = jnp.exp(lr * dt)
    ar = mag * jnp.cos(li * dt)
    ai = mag * jnp.sin(li * dt)
    den = lr * lr + li * li
    kr = ((ar - 1.0) * lr + ai * li) / den
    ki = (ai * lr - (ar - 1.0) * li) / den
    br = b_re.astype(F32)
    bi = b_im.astype(F32)
    bbr = kr[..., None] * br - ki[..., None] * bi
    bbi = kr[..., None] * bi + ki[..., None] * br
    cr = c_re.astype(F32)
    ci = c_im.astype(F32)
    pr = [jnp.ones_like(ar)]
    pi = [jnp.zeros_like(ar)]
    for _ in range(S5_WIN):
        pr, pi = pr + [pr[-1] * ar - pi[-1] * ai], pi + [pr[-1] * ai + pi[-1] * ar]
    w, nq, ng = S5_WIN, QUADS, QUAD_GROUPS

    def lanes(v):
        return v.reshape(nq, 1, QUAD_ST)

    def by_state(v):
        return v.reshape(nq, ng, -1, S5_STATE).transpose(0, 2, 1, 3).reshape(nq, -1, QUAD_ST)

    def block_diag(v, width):
        v = v.astype(BF16)
        lane_group = (lax.broadcasted_iota(jnp.int32, (ng, v.shape[-1]), 1) // width) % ng
        keep = (lane_group == lax.broadcasted_iota(jnp.int32, (ng, v.shape[-1]), 0)).astype(v.dtype)
        out = v[:, :, None] * keep[None, None, :, None, :]
        return out.reshape(nq, -1, v.shape[-1])

    br_t = by_state(jnp.swapaxes(bbr, 1, 2))
    bi_t = by_state(jnp.swapaxes(bbi, 1, 2))
    cr_t = by_state(cr)
    ci_t = by_state(ci)
    pr_l = [lanes(v) for v in pr]
    pi_l = [lanes(v) for v in pi]

    wb = jnp.stack([jnp.concatenate([pr_l[w - 1 - j] * br_t - pi_l[w - 1 - j] * bi_t,
                                     pr_l[w - 1 - j] * bi_t + pi_l[w - 1 - j] * br_t], axis=-1)
                    for j in range(w)], axis=1)
    wb = block_diag(wb, S5_STATE)

    wc = jnp.stack([jnp.concatenate([cr_t * pr_l[r + 1] - ci_t * pi_l[r + 1],
                                     -(cr_t * pi_l[r + 1] + ci_t * pr_l[r + 1])], axis=-1)
                    for r in range(w)], axis=1)
    wc = jnp.swapaxes(block_diag(wc, S5_STATE), 1, 2)

    hp = lax.Precision.HIGHEST
    taps = [jnp.einsum("gop,gpi->gio", cr * pr[d][:, None, :] - ci * pi[d][:, None, :], bbr, precision=hp)
            - jnp.einsum("gop,gpi->gio", cr * pi[d][:, None, :] + ci * pr[d][:, None, :], bbi, precision=hp)
            for d in range(w)]
    taps = [v.reshape(nq, ng, S5_GROUP, S5_GROUP).transpose(0, 2, 1, 3).reshape(nq, S5_GROUP, QUAD_CH)
            for v in taps]
    zero = jnp.zeros_like(taps[0])
    wt = jnp.stack([jnp.concatenate([taps[r - j] if r >= j else zero for r in range(w)], axis=-1)
                    for j in range(w)], axis=1)
    wt = block_diag(wt, S5_GROUP)

    return wb, wc, wt, pr[w].reshape(nq, QUAD_ST), pi[w].reshape(nq, QUAD_ST)


def _s5_layer(h, g, w_in, window_weights, d_skip, w_glu, b_glu, w_out, name, *, head=None, final_g=None):
    first = head is not None
    final = final_g is not None
    fg = _row(final_g) if final else jnp.ones((1, D_MODEL), F32)
    consts = [_row(g), w_in.astype(BF16), *window_weights, _row(d_skip), w_glu.astype(BF16), _row(b_glu),
              w_out.astype(BF16), fg]
    scratch = [pltpu.VMEM((WIN_ROWS, QUADS * QUAD_IN), BF16),
               pltpu.VMEM((WIN_ROWS, 2 * QUADS * QUAD_ST), F32),
               pltpu.VMEM((2, BATCH, QUADS * QUAD_ST), F32)]
    return _call(functools.partial(_s5_kernel, first=first, final=final), name, h, consts, scratch, head=head,
                 batch_major_out=final)


def _conv_layer(h, g, w_in, conv_w, conv_b, w_out, name):
    consts = [_row(g), w_in.astype(BF16), conv_w.astype(F32), _row(conv_b), w_out.astype(BF16)]
    scratch = [pltpu.VMEM((CONV_HIST + ROWS, CONV_E), F32), pltpu.VMEM((ROWS, CONV_E), BF16)]
    return _call(_conv_kernel, name, h, consts, scratch)


def _pool_layer(h, g, w_in, w_grp, b_grp, scale, w_out, name):
    consts = [_row(g), w_in.astype(BF16), w_grp.astype(BF16), b_grp.astype(F32), _row(scale),
              w_out.astype(BF16)]
    scratch = [pltpu.VMEM((HIST + ROWS, POOL_E), F32), pltpu.VMEM((ROWS, POOL_E), BF16)]
    return _call(_pool_kernel, name, h, consts, scratch)


def kernel(x, meta_tokens, norm0_g, l0_w_in, l0_lam_re, l0_lam_im, l0_log_dt, l0_b_re, l0_b_im, l0_c_re, l0_c_im, l0_d_skip, l0_w_glu, l0_b_glu, l0_w_out, norm1_g, l1_w_in, l1_conv_w, l1_conv_b, l1_w_out, norm2_g, l2_w_in, l2_w_grp, l2_b_grp, l2_scale, l2_w_out, norm3_g, l3_w_in, l3_lam_re, l3_lam_im, l3_log_dt, l3_b_re, l3_b_im, l3_c_re, l3_c_im, l3_d_skip, l3_w_glu, l3_b_glu, l3_w_out, final_g):
    assert x.shape == (BATCH, SEQ, D_MODEL) and meta_tokens.shape == (N_META, D_MODEL)
    head = jnp.broadcast_to(meta_tokens.astype(x.dtype)[:, None, :], (N_META, BATCH, D_MODEL))
    head = head.reshape(N_META * BATCH, D_MODEL)
    s5_params = [(l0_lam_re, l3_lam_re), (l0_lam_im, l3_lam_im), (l0_log_dt, l3_log_dt), (l0_b_re, l3_b_re),
                 (l0_b_im, l3_b_im), (l0_c_re, l3_c_re), (l0_c_im, l3_c_im)]
    stacked = jax.vmap(_s5_weights)(*[jnp.stack(p) for p in s5_params])
    h = _s5_layer(x, norm0_g, l0_w_in, [(w, 0) for w in stacked], l0_d_skip, l0_w_glu, l0_b_glu, l0_w_out,
                  "s5_layer0", head=head)
    h = _conv_layer(h, norm1_g, l1_w_in, l1_conv_w, l1_conv_b, l1_w_out, "conv_layer1")
    h = _pool_layer(h, norm2_g, l2_w_in, l2_w_grp, l2_b_grp, l2_scale, l2_w_out, "pool_layer2")
    return _s5_layer(h, norm3_g, l3_w_in, [(w, 1) for w in stacked], l3_d_skip, l3_w_glu, l3_b_glu, l3_w_out,
                     "s5_layer3", final_g=final_g)
```

```python
import functools

import jax
import jax.numpy as jnp
from jax import lax
from jax.experimental import pallas as pl
from jax.experimental.pallas import tpu as pltpu

D_MODEL = 1024
BATCH = 8
SEQ = 4096
N_META = 16
EPS = 1e-6
S5_GROUP = 16
S5_GROUPS = 64
S5_STATE = 64
CONV_E = 2048
CONV_K = 3
POOL_E = 2048
POOL_WINDOWS = (2, 4, 8, 16)
POOL_GROUP = 512

TQ = 128
ROWS = TQ * BATCH
PAD = TQ - N_META
L_PAD = SEQ + TQ
N_CHUNKS = L_PAD // TQ
LANES = 128
S5_WIN = 4
WIN_ROWS = ROWS // S5_WIN
QUAD_GROUPS = 4
QUADS = S5_GROUPS // QUAD_GROUPS
QUAD_CH = QUAD_GROUPS * S5_GROUP
QUAD_ST = QUAD_GROUPS * S5_STATE
QUAD_IN = S5_WIN * QUAD_CH
S5_AHEAD = 3
TAIL_PARTS = 2
assert QUAD_CH * 2 == LANES and S5_WIN == 4
HIST = max(POOL_WINDOWS) * BATCH
CONV_HIST = (CONV_K - 1) * BATCH
assert CONV_K == 3
CONV_BLOCK = 256
V7X_VMEM_BYTES = 64 * 1024 * 1024
VMEM_LIMIT = V7X_VMEM_BYTES * 31 // 32
assert N_META % (S5_WIN * TAIL_PARTS) == 0

F32 = jnp.float32
BF16 = jnp.bfloat16


def _rms(h, g):
    ms = jnp.mean(h * h, axis=-1, keepdims=True)
    return h * lax.rsqrt(ms + EPS) * g


def _dot(a, b):
    return jnp.dot(a, b, preferred_element_type=F32)


def _swap_halves(v):
    return pltpu.roll(v, LANES // 2, 1)


def _s5_kernel(*refs, first, final):
    if first:
        head_ref, refs = refs[0], refs[1:]
    (h_ref, g_ref, win_ref, wb_ref, wc_ref, wt_ref, awr_ref, awi_ref, dsk_ref,
     wglu_ref, bglu_ref, wout_ref, fg_ref, o_ref, uw_ref, xs_ref, carry_ref) = refs
    i = pl.program_id(0)

    def run(tq, h, store):
        rows, win_rows, n_win = tq * BATCH, tq * BATCH // S5_WIN, tq // S5_WIN
        nb = _rms(h, g_ref[...]).astype(BF16)
        uz = _dot(nb, win_ref[...])
        u = uz[:, :D_MODEL]
        z = uz[:, D_MODEL:]

        low = lax.broadcasted_iota(jnp.int32, (win_rows, LANES), 1) < QUAD_CH
        ut = u.reshape(n_win, S5_WIN, BATCH, D_MODEL)
        for jj in range(S5_WIN // 2):
            ta = ut[:, 2 * jj].reshape(win_rows, D_MODEL)
            tb = ut[:, 2 * jj + 1].reshape(win_rows, D_MODEL)
            for col in range(D_MODEL // LANES):
                a = ta[:, col * LANES:(col + 1) * LANES]
                b = tb[:, col * LANES:(col + 1) * LANES]
                even = jnp.where(low, a, _swap_halves(b))
                odd = jnp.where(low, _swap_halves(a), b)
                uw_ref[0:win_rows, 2 * col * QUAD_IN + jj * LANES:2 * col * QUAD_IN + (jj + 1) * LANES] = (
                    even.astype(BF16))
                uw_ref[0:win_rows, (2 * col + 1) * QUAD_IN + jj * LANES:(2 * col + 1) * QUAD_IN + (jj + 1) * LANES] = (
                    odd.astype(BF16))

        def window_inputs(q):
            xs_ref[0:win_rows, 2 * q * QUAD_ST:2 * (q + 1) * QUAD_ST] = _dot(
                uw_ref[0:win_rows, q * QUAD_IN:(q + 1) * QUAD_IN], wb_ref[q])

        for q in range(S5_AHEAD):
            window_inputs(q)
        yw = []
        for q in range(QUADS):
            if q + S5_AHEAD < QUADS:
                window_inputs(q + S5_AHEAD)
            st = 2 * q * QUAD_ST
            uq = uw_ref[0:win_rows, q * QUAD_IN:(q + 1) * QUAD_IN]
            awr = jnp.broadcast_to(awr_ref[q:q + 1, :], (BATCH, QUAD_ST))
            awi = jnp.broadcast_to(awi_ref[q:q + 1, :], (BATCH, QUAD_ST))
            sr = carry_ref[0, :, q * QUAD_ST:(q + 1) * QUAD_ST]
            si = carry_ref[1, :, q * QUAD_ST:(q + 1) * QUAD_ST]
            for m in range(n_win):
                r = m * BATCH
                xr = xs_ref[r:r + BATCH, st:st + QUAD_ST]
                xi = xs_ref[r:r + BATCH, st + QUAD_ST:st + 2 * QUAD_ST]
                xs_ref[r:r + BATCH, st:st + QUAD_ST] = sr
                xs_ref[r:r + BATCH, st + QUAD_ST:st + 2 * QUAD_ST] = si
                sr, si = awr * sr - awi * si + xr, awr * si + awi * sr + xi
            carry_ref[0, :, q * QUAD_ST:(q + 1) * QUAD_ST] = sr
            carry_ref[1, :, q * QUAD_ST:(q + 1) * QUAD_ST] = si
            if store is None:
                continue
            s_in = xs_ref[0:win_rows, st:st + 2 * QUAD_ST].astype(BF16)
            yw.append(_dot(s_in, wc_ref[q]) + _dot(uq, wt_ref[q]))
        if store is None:
            return

        cols = [[None] * (D_MODEL // LANES) for _ in range(S5_WIN)]
        for col in range(D_MODEL // LANES):
            for jj in range(S5_WIN // 2):
                p = yw[2 * col][:, jj * LANES:(jj + 1) * LANES]
                q = yw[2 * col + 1][:, jj * LANES:(jj + 1) * LANES]
                cols[2 * jj][col] = jnp.where(low, p, _swap_halves(q))
                cols[2 * jj + 1][col] = jnp.where(low, _swap_halves(p), q)
        yt = [jnp.concatenate(c, axis=1).reshape(n_win, BATCH, D_MODEL) for c in cols]
        y = jnp.stack(yt, axis=1).reshape(rows, D_MODEL) + dsk_ref[...] * u
        part_t = tq // TAIL_PARTS
        parts = [slice(k * part_t * BATCH, (k + 1) * part_t * BATCH) for k in range(TAIL_PARTS)]
        gelus = [jax.nn.gelu(y[p]) for p in parts]
        gates = [_dot(yg.astype(BF16), wglu_ref[...]) for yg in gelus]
        for k, p in enumerate(parts):
            yh = gelus[k] * jax.nn.sigmoid(gates[k] + bglu_ref[...])
            yh = yh * jax.nn.silu(z[p])
            hn = h[p] + _dot(yh.astype(BF16), wout_ref[...])
            if final:
                hn = _rms(hn, fg_ref[...])
            store(k * part_t, part_t, hn)

    def store_rows(t0):
        def store(t, nt, hn):
            o_ref[(t0 + t) * BATCH:(t0 + t + nt) * BATCH, :] = hn
        return store

    def store_batch_major(t, nt, hn):
        o_ref[:, t:t + nt, :] = jnp.swapaxes(hn.reshape(nt, BATCH, D_MODEL), 0, 1)

    @pl.when(i == 0)
    def _():
        carry_ref[...] = jnp.zeros_like(carry_ref)
        h = head_ref[...] if first else h_ref[PAD * BATCH:ROWS, :]
        if final:
            run(N_META, h, None)
        else:
            o_ref[0:PAD * BATCH, :] = jnp.zeros((PAD * BATCH, D_MODEL), F32)
            run(N_META, h, store_rows(PAD))

    @pl.when(i > 0)
    def _():
        if first:
            h = jnp.swapaxes(h_ref[...], 0, 1).reshape(ROWS, D_MODEL)
        else:
            h = h_ref[...]
        run(TQ, h, store_batch_major if final else store_rows(0))


def _conv_kernel(h_ref, g_ref, win_ref, cw_ref, cb_ref, wout_ref, o_ref, hc_ref, y_ref):
    @pl.when(pl.program_id(0) == 0)
    def _():
        hc_ref[0:CONV_HIST, :] = jnp.zeros((CONV_HIST, CONV_E), F32)

    h = h_ref[...]
    nb = _rms(h, g_ref[...]).astype(BF16)
    cw = CONV_BLOCK
    for c in range(CONV_E // cw):
        lo = c * cw
        bg = _dot(nb, win_ref[:, lo:lo + cw])
        cg = _dot(nb, win_ref[:, CONV_E + lo:CONV_E + lo + cw])
        v = _dot(nb, win_ref[:, 2 * CONV_E + lo:2 * CONV_E + lo + cw])
        z = _dot(nb, win_ref[:, 3 * CONV_E + lo:3 * CONV_E + lo + cw])
        hc = cg * v
        hc_ref[CONV_HIST:CONV_HIST + ROWS, lo:lo + cw] = hc
        conv = (cw_ref[2:3, lo:lo + cw] * hc
                + cw_ref[1:2, lo:lo + cw] * hc_ref[BATCH:BATCH + ROWS, lo:lo + cw]
                + cw_ref[0:1, lo:lo + cw] * hc_ref[0:ROWS, lo:lo + cw]
                + cb_ref[:, lo:lo + cw])
        hc_ref[0:CONV_HIST, lo:lo + cw] = hc[ROWS - CONV_HIST:ROWS]
        y_ref[:, lo:lo + cw] = (bg * conv * jax.nn.silu(z)).astype(BF16)
    o_ref[...] = h + _dot(y_ref[...], wout_ref[...])


def _window_sum(ref, lo, width, w):
    a = ref[HIST - (w - 1) * BATCH:HIST + ROWS, lo:lo + width]
    span = 1
    while span < w:
        sh = span * BATCH
        a = a[sh:] + a[:-sh]
        span *= 2
    return a


def _pool_kernel(h_ref, g_ref, win_ref, wgrp_ref, bgrp_ref, scale_ref, wout_ref, o_ref, ub_ref, y_ref):
    i = pl.program_id(0)

    @pl.when(i == 0)
    def _():
        ub_ref[0:HIST, :] = jnp.zeros((HIST, POOL_E), F32)

    h = h_ref[...]
    nb = _rms(h, g_ref[...]).astype(BF16)
    pos = i * TQ - PAD + 1 + (lax.broadcasted_iota(jnp.int32, (ROWS, POOL_GROUP), 0) // BATCH)
    for k, w in enumerate(POOL_WINDOWS):
        lo = k * POOL_GROUP
        u = _dot(nb, win_ref[:, lo:lo + POOL_GROUP])
        z = _dot(nb, win_ref[:, POOL_E + lo:POOL_E + lo + POOL_GROUP])
        ub_ref[HIST:HIST + ROWS, lo:lo + POOL_GROUP] = u
        ws = _window_sum(ub_ref, lo, POOL_GROUP, w)
        ub_ref[0:HIST, lo:lo + POOL_GROUP] = u[ROWS - HIST:ROWS]
        cnt = jnp.clip(pos, 1, w).astype(F32)
        mixed = ws / cnt - u
        o = _dot(mixed.astype(BF16), wgrp_ref[k]) + bgrp_ref[k:k + 1, :]
        y = o * scale_ref[:, lo:lo + POOL_GROUP]
        y_ref[:, lo:lo + POOL_GROUP] = (y * jax.nn.silu(z)).astype(BF16)
    o_ref[...] = h + _dot(y_ref[...], wout_ref[...])


def _const_spec(c):
    if isinstance(c, tuple):
        arr, layer = c
        tail = arr.ndim - 1
        return pl.BlockSpec((None,) + arr.shape[1:], lambda i: (layer,) + (0,) * tail, pipeline_mode=pl.Buffered(1))
    nd = c.ndim
    return pl.BlockSpec(c.shape, lambda i: (0,) * nd, pipeline_mode=pl.Buffered(1))


_TIME_MAJOR = pl.BlockSpec((ROWS, D_MODEL), lambda i: (i, 0))
_BATCH_MAJOR = pl.BlockSpec((BATCH, TQ, D_MODEL), lambda i: (0, jnp.maximum(i - 1, 0), 0))


def _call(body, name, h, consts, scratch, *, head=None, batch_major_out=False):
    pre = [] if head is None else [head]
    out_shape = (BATCH, SEQ, D_MODEL) if batch_major_out else (L_PAD * BATCH, D_MODEL)
    return pl.pallas_call(
        body,
        out_shape=jax.ShapeDtypeStruct(out_shape, h.dtype),
        grid=(N_CHUNKS,),
        in_specs=([_const_spec(p) for p in pre] + [_BATCH_MAJOR if pre else _TIME_MAJOR]
                  + [_const_spec(c) for c in consts]),
        out_specs=_BATCH_MAJOR if batch_major_out else _TIME_MAJOR,
        scratch_shapes=scratch,
        compiler_params=pltpu.CompilerParams(
            dimension_semantics=("arbitrary",), vmem_limit_bytes=VMEM_LIMIT,
            allow_input_fusion=[False] * (len(pre) + 1) + [True] * len(consts)),
        name=name,
    )(*pre, h, *[c[0] if isinstance(c, tuple) else c for c in consts])


def _row(v):
    return v.astype(F32).reshape(1, -1)


def _s5_weights(lam_re, lam_im, log_dt, b_re, b_im, c_re, c_im):
    lr = lam_re.astype(F32)
    li = lam_im.astype(F32)
    dt = jnp.exp(log_dt.astype(F32))[:, None]
    mag = jnp.exp(lr * dt)
    ar = mag * jnp.cos(li * dt)
    ai = mag * jnp.sin(li * dt)
    den = lr * lr + li * li
    kr = ((ar - 1.0) * lr + ai * li) / den
    ki = (ai * lr - (ar - 1.0) * li) / den
    br = b_re.astype(F32)
    bi = b_im.astype(F32)
    bbr = kr[..., None] * br - ki[..., None] * bi
    bbi = kr[..., None] * bi + ki[..., None] * br
    cr = c_re.astype(F32)
    ci = c_im.astype(F32)
    pr = [jnp.ones_like(ar)]
    pi = [jnp.zeros_like(ar)]
    for _ in range(S5_WIN):
        pr, pi = pr + [pr[-1] * ar - pi[-1] * ai], pi + [pr[-1] * ai + pi[-1] * ar]
    w, nq, ng = S5_WIN, QUADS, QUAD_GROUPS

    def lanes(v):
        return v.reshape(nq, 1, QUAD_ST)

    def by_state(v):
        return v.reshape(nq, ng, -1, S5_STATE).transpose(0, 2, 1, 3).reshape(nq, -1, QUAD_ST)

    def block_diag(v, width):
        v = v.astype(BF16)
        lane_group = (lax.broadcasted_iota(jnp.int32, (ng, v.shape[-1]), 1) // width) % ng
        keep = (lane_group == lax.broadcasted_iota(jnp.int32, (ng, v.shape[-1]), 0)).astype(v.dtype)
        out = v[:, :, None] * keep[None, None, :, None, :]
        return out.reshape(nq, -1, v.shape[-1])

    br_t = by_state(jnp.swapaxes(bbr, 1, 2))
    bi_t = by_state(jnp.swapaxes(bbi, 1, 2))
    cr_t = by_state(cr)
    ci_t = by_state(ci)
    pr_l = [lanes(v) for v in pr]
    pi_l = [lanes(v) for v in pi]

    wb = jnp.stack([jnp.concatenate([pr_l[w - 1 - j] * br_t - pi_l[w - 1 - j] * bi_t,
                                     pr_l[w - 1 - j] * bi_t + pi_l[w - 1 - j] * br_t], axis=-1)
                    for j in range(w)], axis=1)
    wb = block_diag(wb, S5_STATE)

    wc = jnp.stack([jnp.concatenate([cr_t * pr_l[r + 1] - ci_t * pi_l[r + 1],
                                     -(cr_t * pi_l[r + 1] + ci_t * pr_l[r + 1])], axis=-1)
                    for r in range(w)], axis=1)
    wc = jnp.swapaxes(block_diag(wc, S5_STATE), 1, 2)

    hp = lax.Precision.HIGHEST
    taps = [jnp.einsum("gop,gpi->gio", cr * pr[d][:, None, :] - ci * pi[d][:, None, :], bbr, precision=hp)
            - jnp.einsum("gop,gpi->gio", cr * pi[d][:, None, :] + ci * pr[d][:, None, :], bbi, precision=hp)
            for d in range(w)]
    taps = [v.reshape(nq, ng, S5_GROUP, S5_GROUP).transpose(0, 2, 1, 3).reshape(nq, S5_GROUP, QUAD_CH)
            for v in taps]
    zero = jnp.zeros_like(taps[0])
    wt = jnp.stack([jnp.concatenate([taps[r - j] if r >= j else zero for r in range(w)], axis=-1)
                    for j in range(w)], axis=1)
    wt = block_diag(wt, S5_GROUP)

    return wb, wc, wt, pr[w].reshape(nq, QUAD_ST), pi[w].reshape(nq, QUAD_ST)


def _s5_layer(h, g, w_in, window_weights, d_skip, w_glu, b_glu, w_out, name, *, head=None, final_g=None):
    first = head is not None
    final = final_g is not None
    fg = _row(final_g) if final else jnp.ones((1, D_MODEL), F32)
    consts = [_row(g), w_in.astype(BF16), *window_weights, _row(d_skip), w_glu.astype(BF16), _row(b_glu),
              w_out.astype(BF16), fg]
    scratch = [pltpu.VMEM((WIN_ROWS, QUADS * QUAD_IN), BF16),
               pltpu.VMEM((WIN_ROWS, 2 * QUADS * QUAD_ST), F32),
               pltpu.VMEM((2, BATCH, QUADS * QUAD_ST), F32)]
    return _call(functools.partial(_s5_kernel, first=first, final=final), name, h, consts, scratch, head=head,
                 batch_major_out=final)


def _conv_layer(h, g, w_in, conv_w, conv_b, w_out, name):
    consts = [_row(g), w_in.astype(BF16), conv_w.astype(F32), _row(conv_b), w_out.astype(BF16)]
    scratch = [pltpu.VMEM((CONV_HIST + ROWS, CONV_E), F32), pltpu.VMEM((ROWS, CONV_E), BF16)]
    return _call(_conv_kernel, name, h, consts, scratch)


def _pool_layer(h, g, w_in, w_grp, b_grp, scale, w_out, name):
    consts = [_row(g), w_in.astype(BF16), w_grp.astype(BF16), b_grp.astype(F32), _row(scale),
              w_out.astype(BF16)]
    scratch = [pltpu.VMEM((HIST + ROWS, POOL_E), F32), pltpu.VMEM((ROWS, POOL_E), BF16)]
    return _call(_pool_kernel, name, h, consts, scratch)


def kernel(x, meta_tokens, norm0_g, l0_w_in, l0_lam_re, l0_lam_im, l0_log_dt, l0_b_re, l0_b_im, l0_c_re, l0_c_im, l0_d_skip, l0_w_glu, l0_b_glu, l0_w_out, norm1_g, l1_w_in, l1_conv_w, l1_conv_b, l1_w_out, norm2_g, l2_w_in, l2_w_grp, l2_b_grp, l2_scale, l2_w_out, norm3_g, l3_w_in, l3_lam_re, l3_lam_im, l3_log_dt, l3_b_re, l3_b_im, l3_c_re, l3_c_im, l3_d_skip, l3_w_glu, l3_b_glu, l3_w_out, final_g):
    assert x.shape == (BATCH, SEQ, D_MODEL) and meta_tokens.shape == (N_META, D_MODEL)
    head = jnp.broadcast_to(meta_tokens.astype(x.dtype)[:, None, :], (N_META, BATCH, D_MODEL))
    head = head.reshape(N_META * BATCH, D_MODEL)
    s5_params = [(l0_lam_re, l3_lam_re), (l0_lam_im, l3_lam_im), (l0_log_dt, l3_log_dt), (l0_b_re, l3_b_re),
                 (l0_b_im, l3_b_im), (l0_c_re, l3_c_re), (l0_c_im, l3_c_im)]
    stacked = jax.vmap(_s5_weights)(*[jnp.stack(p) for p in s5_params])
    h = _s5_layer(x, norm0_g, l0_w_in, [(w, 0) for w in stacked], l0_d_skip, l0_w_glu, l0_b_glu, l0_w_out,
                  "s5_layer0", head=head)
    h = _conv_layer(h, norm1_g, l1_w_in, l1_conv_w, l1_conv_b, l1_w_out, "conv_layer1")
    h = _pool_layer(h, norm2_g, l2_w_in, l2_w_grp, l2_b_grp, l2_scale, l2_w_out, "pool_layer2")
    return _s5_layer(h, norm3_g, l3_w_in, [(w, 1) for w in stacked], l3_d_skip, l3_w_glu, l3_b_glu, l3_w_out,
                     "s5_layer3", final_g=final_g)
```
